```python
import math
import jax, jax.numpy as jnp
from jax import lax
import numpy as np

D_MODEL = 4096
BATCH = 4
SEQ = 4096
DEPTH = 1

RW_HEADS = 32
RW_HEAD = 64
RW_WIDTH = RW_HEADS * RW_HEAD
RW_DECAY_LORA = 96
RW_A_LORA = 96
RW_GATE_LORA = 256
RW_GN_EPS = 64e-5
RW_DECAY_SCALE = 0.606531

DN_HEADS = 16
DN_HEAD_K = 128
DN_HEAD_V = 128
DN_KEY = DN_HEADS * DN_HEAD_K
DN_VAL = DN_HEADS * DN_HEAD_V
DN_QKV = 2 * DN_KEY + DN_VAL
DN_CONV = 4
DN_CHUNK = 64

N_BRANCH = 2
RW_COLS = 3 * RW_WIDTH + RW_DECAY_LORA + RW_A_LORA + RW_GATE_LORA
DN_COLS = DN_QKV + 2 * DN_HEADS + DN_VAL
GATE_COLS = N_BRANCH * D_MODEL
IN_COLS = RW_COLS + DN_COLS + GATE_COLS

N_GROUPS = 8
EXPERTS_PER_GROUP = 8
N_EXPERTS = N_GROUPS * EXPERTS_PER_GROUP
TOP_K = 2
D_EXPERT = 768
MOE_BLOCK = 128

NORM_EPS = 1e-6

kernel_name = 'hybrid_rwkv7_gdn_hmoe_block'


def _split(t, sizes):
    idx = np.cumsum(np.array(sizes))[:-1].tolist()
    return jnp.split(t, idx, axis=-1)


def _rmsnorm(x, g):
    xf = x.astype(jnp.float32)
    y = xf * lax.rsqrt(jnp.mean(xf * xf, axis=-1, keepdims=True) + NORM_EPS)
    return (y * g.astype(jnp.float32)).astype(x.dtype)


def _l2norm(t):
    return t * lax.rsqrt(jnp.sum(t * t, axis=-1, keepdims=True) + NORM_EPS)


def _rwkv7_time_mix(h, mu, w0, w2, a0, a2, g2, k_k, k_a, r_k, ln_w, ln_b):
    B, S, _ = h.shape
    f32 = jnp.float32
    hd = (RW_HEADS, RW_HEAD)
    h_prev = jnp.pad(h, ((0, 0), (1, 0), (0, 0)))[:, :-1]
    hm = h + (h_prev - h) * mu
    r, wd, k, v, ad, gd = _split(hm, (RW_WIDTH, RW_DECAY_LORA, RW_WIDTH, RW_WIDTH, RW_A_LORA, RW_GATE_LORA))
    log_w = -RW_DECAY_SCALE * jax.nn.sigmoid((w0 + jnp.tanh(wd) @ w2).astype(f32))
    a = jax.nn.sigmoid((a0 + ad @ a2).astype(f32))
    g = (jax.nn.sigmoid(gd) @ g2).astype(f32)
    heads = lambda t: t.astype(f32).reshape(B, S, RW_HEADS, RW_HEAD)
    r_h = heads(r)
    v_h = heads(v)
    a_h = heads(a)
    w_h = jnp.exp(heads(log_w))
    kk = _l2norm(heads(k) * k_k.astype(f32).reshape(hd))
    k_h = heads(k) * (1.0 + (a_h - 1.0) * k_a.astype(f32).reshape(hd))

    def step(state, inp):
        r_t, w_t, k_t, v_t, kk_t, a_t = inp
        sk = jnp.einsum('bhvk,bhk->bhv', state, kk_t)
        state = (state * w_t[:, :, None, :]
                 - sk[..., None] * (kk_t * a_t)[:, :, None, :]
                 + v_t[..., None] * k_t[:, :, None, :])
        return state, jnp.einsum('bhvk,bhk->bhv', state, r_t)

    tm = lambda t: jnp.moveaxis(t, 1, 0)
    s0 = jnp.zeros((B, RW_HEADS, RW_HEAD, RW_HEAD), f32)
    _, o = lax.scan(step, s0, (tm(r_h), tm(w_h), tm(k_h), tm(v_h), tm(kk), tm(a_h)))
    o = jnp.moveaxis(o, 0, 1)
    mean = jnp.mean(o, axis=-1, keepdims=True)
    var = jnp.mean(jnp.square(o - mean), axis=-1, keepdims=True)
    o = (o - mean) * lax.rsqrt(var + RW_GN_EPS) * ln_w.astype(f32).reshape(hd) + ln_b.astype(f32).reshape(hd)
    bonus = jnp.sum(r_h * k_h * r_k.astype(f32), axis=-1, keepdims=True) * v_h
    o = (o + bonus).reshape(B, S, RW_WIDTH) * g
    return o.astype(h.dtype)


def _chunk_gated_delta_rule(q, k, v, beta, g):
    B, S, H, DK = q.shape
    DV = v.shape[-1]
    C = DN_CHUNK
    NC = S // C

    def chunks(t):
        t = jnp.moveaxis(t, 2, 1)
        return t.reshape((B, H, NC, C) + t.shape[3:])

    q, k, v, beta, g = chunks(q), chunks(k), chunks(v), chunks(beta), chunks(g)
    G = jnp.cumsum(g, axis=-1)
    causal = jnp.tril(jnp.ones((C, C), bool))
    strict = jnp.tril(jnp.ones((C, C), bool), -1)
    diff = G[..., :, None] - G[..., None, :]
    decay = jnp.where(causal, jnp.exp(jnp.where(causal, diff, 0.0)), 0.0)
    kb = k * beta[..., None]
    Lmat = jnp.where(strict, jnp.einsum('bhnik,bhnjk->bhnij', kb, k) * decay, 0.0)
    M = Lmat + jnp.eye(C, dtype=Lmat.dtype)
    rhs = jnp.concatenate([v * beta[..., None], kb * jnp.exp(G)[..., None]], axis=-1)
    sol = lax.linalg.triangular_solve(M, rhs, left_side=True, lower=True, unit_diagonal=True)
    u, w = sol[..., :DV], sol[..., DV:]
    attn = jnp.einsum('bhnik,bhnjk->bhnij', q, k) * decay
    q_dec = q * jnp.exp(G)[..., None]
    G_last = G[..., -1:]
    k_tail = k * jnp.exp(G_last - G)[..., None]
    chunk_decay = jnp.exp(G_last)[..., None]

    def step(state, inp):
        u_c, w_c, qd_c, a_c, kt_c, cd_c = inp
        v_new = u_c - jnp.einsum('bhck,bhkv->bhcv', w_c, state)
        o_c = jnp.einsum('bhck,bhkv->bhcv', qd_c, state) + jnp.einsum('bhij,bhjv->bhiv', a_c, v_new)
        state = state * cd_c + jnp.einsum('bhck,bhcv->bhkv', kt_c, v_new)
        return state, o_c

    nm = lambda t: jnp.moveaxis(t, 2, 0)
    s0 = jnp.zeros((B, H, DK, DV), q.dtype)
    _, o = lax.scan(step, s0, (nm(u), nm(w), nm(q_dec), nm(attn), nm(k_tail), nm(chunk_decay)))
    o = jnp.moveaxis(o, 0, 2).reshape(B, H, S, DV)
    return jnp.moveaxis(o, 1, 2)


def _gated_deltanet(h, conv_w, a_log, dt_bias, norm_w):
    B, S, _ = h.shape
    f32 = jnp.float32
    qkv, b, al, z = _split(h, (DN_QKV, DN_HEADS, DN_HEADS, DN_VAL))
    qkv = lax.conv_general_dilated(qkv, conv_w[:, None, :], (1,), [(DN_CONV - 1, 0)],
                                   dimension_numbers=('NWC', 'WIO', 'NWC'),
                                   feature_group_count=DN_QKV)
    qkv = jax.nn.silu(qkv)
    q, k, v = _split(qkv, (DN_KEY, DN_KEY, DN_VAL))
    q = _l2norm(q.astype(f32).reshape(B, S, DN_HEADS, DN_HEAD_K)) * (DN_HEAD_K ** -0.5)
    k = _l2norm(k.astype(f32).reshape(B, S, DN_HEADS, DN_HEAD_K))
    v = v.astype(f32).reshape(B, S, DN_HEADS, DN_HEAD_V)
    beta = jax.nn.sigmoid(b.astype(f32))
    g = -jnp.exp(a_log.astype(f32)) * jax.nn.softplus(al.astype(f32) + dt_bias.astype(f32))
    o = _chunk_gated_delta_rule(q, k, v, beta, g)
    o = o * lax.rsqrt(jnp.mean(o * o, axis=-1, keepdims=True) + NORM_EPS) * norm_w.astype(f32)
    o = o * jax.nn.silu(z.astype(f32).reshape(B, S, DN_HEADS, DN_HEAD_V))
    return o.reshape(B, S, DN_VAL).astype(h.dtype)


def _hier_moe(h, gr_w, gr_b, er_w, er_b, w1, w3, w2):
    B, S, D = h.shape
    T = B * S
    A = T * TOP_K
    xt = h.reshape(T, D)
    gprob = jax.nn.softmax((xt @ gr_w + gr_b).astype(jnp.float32), axis=-1)
    gp, gi = lax.top_k(gprob, 1)
    elog_all = (xt @ er_w + er_b).astype(jnp.float32).reshape(T, N_GROUPS, EXPERTS_PER_GROUP)
    elog = elog_all[jnp.arange(T), gi[:, 0]]
    ep, ei = lax.top_k(jax.nn.softmax(elog, axis=-1), TOP_K)
    weight = gp * ep / jnp.sum(ep, axis=-1, keepdims=True)
    eid = gi * EXPERTS_PER_GROUP + ei

    e_flat = eid.reshape(A)
    w_flat = weight.reshape(A).astype(h.dtype)
    tok_flat = jnp.repeat(jnp.arange(T, dtype=jnp.int32), TOP_K)
    order = jnp.argsort(e_flat)
    se, stok, sw = e_flat[order], tok_flat[order], w_flat[order]
    counts = jnp.zeros((N_EXPERTS,), jnp.int32).at[e_flat].add(1)
    offs = jnp.cumsum(counts) - counts
    pcounts = (counts + MOE_BLOCK - 1) // MOE_BLOCK * MOE_BLOCK
    pend = jnp.cumsum(pcounts)
    poffs = pend - pcounts
    dest = poffs[se] + (jnp.arange(A, dtype=jnp.int32) - offs[se])
    P = A + N_EXPERTS * MOE_BLOCK
    n_blocks = P // MOE_BLOCK
    buf_tok = jnp.full((P,), T, jnp.int32).at[dest].set(stok)
    buf_w = jnp.zeros((P,), h.dtype).at[dest].set(sw)
    block_e = jnp.minimum(jnp.searchsorted(pend, jnp.arange(n_blocks) * MOE_BLOCK, side='right'),
                          N_EXPERTS - 1).astype(jnp.int32)
    xpad = jnp.concatenate([xt, jnp.zeros((1, D), xt.dtype)], axis=0)

    def block_fn(args):
        tok_b, w_b, e_b = args
        xb = xpad[tok_b]
        act = jax.nn.silu(xb @ w1[e_b]) * (xb @ w3[e_b])
        return (act @ w2[e_b]) * w_b[:, None]

    yb = lax.map(block_fn, (buf_tok.reshape(n_blocks, MOE_BLOCK),
                            buf_w.reshape(n_blocks, MOE_BLOCK), block_e))
    out = jnp.zeros((T + 1, D), h.dtype).at[buf_tok].add(yb.reshape(P, D))[:T]
    return out.reshape(B, S, D)


def setup_inputs(seed: int = 0) -> dict:
    key = jax.random.key(seed)
    ks = iter(jax.random.split(key, 40))
    f32 = jnp.float32
    L = DEPTH

    def nrm(shape, scale):
        return jax.random.normal(next(ks), shape, f32) * scale

    def unif(shape, lo, hi):
        return jax.random.uniform(next(ks), shape, f32, lo, hi)

    dt = jnp.exp(unif((L, DN_HEADS), math.log(1e-3), math.log(1e-1)))
    return {
        'x': nrm((BATCH, SEQ, D_MODEL), 1.0),
        'norm1_g': 1.0 + nrm((L, D_MODEL), 0.02),
        'w_in': nrm((L, D_MODEL, IN_COLS), D_MODEL ** -0.5),
        'rw_mu': unif((L, RW_COLS), 0.0, 1.0),
        'rw_w0': unif((L, RW_WIDTH), -3.0, 1.0),
        'rw_w2': nrm((L, RW_DECAY_LORA, RW_WIDTH), 0.1 * RW_DECAY_LORA ** -0.5),
        'rw_a0': nrm((L, RW_WIDTH), 0.5),
        'rw_a2': nrm((L, RW_A_LORA, RW_WIDTH), 0.5 * RW_A_LORA ** -0.5),
        'rw_g2': nrm((L, RW_GATE_LORA, RW_WIDTH), RW_GATE_LORA ** -0.5),
        'rw_k_k': 0.85 + nrm((L, RW_WIDTH), 0.05),
        'rw_k_a': 1.0 + nrm((L, RW_WIDTH), 0.05),
        'rw_r_k': nrm((L, RW_HEADS, RW_HEAD), 0.1),
        'rw_ln_w': 1.0 + nrm((L, RW_WIDTH), 0.02),
        'rw_ln_b': nrm((L, RW_WIDTH), 0.02),
        'dn_conv_w': nrm((L, DN_CONV, DN_QKV), DN_CONV ** -0.5),
        'dn_a_log': jnp.log(unif((L, DN_HEADS), 1.0, 16.0)),
        'dn_dt_bias': dt + jnp.log(-jnp.expm1(-dt)),
        'dn_norm_w': 1.0 + nrm((L, DN_HEAD_V), 0.02),
        'gate_b': nrm((L, GATE_COLS), 0.1),
        'w_branch': nrm((L, RW_WIDTH + DN_VAL, D_MODEL), RW_WIDTH ** -0.5),
        'w_out': nrm((L, D_MODEL, D_MODEL), D_MODEL ** -0.5),
        'norm2_g': 1.0 + nrm((L, D_MODEL), 0.02),
        'moe_gr_w': nrm((L, D_MODEL, N_GROUPS), D_MODEL ** -0.5),
        'moe_gr_b': nrm((L, N_GROUPS), 0.01),
        'moe_er_w': nrm((L, D_MODEL, N_EXPERTS), D_MODEL ** -0.5),
        'moe_er_b': nrm((L, N_EXPERTS), 0.01),
        'moe_w1': nrm((L, N_EXPERTS, D_MODEL, D_EXPERT), D_MODEL ** -0.5),
        'moe_w3': nrm((L, N_EXPERTS, D_MODEL, D_EXPERT), D_MODEL ** -0.5),
        'moe_w2': nrm((L, N_EXPERTS, D_EXPERT, D_MODEL), D_EXPERT ** -0.5),
        'final_g': 1.0 + nrm((D_MODEL,), 0.02),
    }


def reference(x, norm1_g, w_in, rw_mu, rw_w0, rw_w2, rw_a0, rw_a2, rw_g2, rw_k_k, rw_k_a,
              rw_r_k, rw_ln_w, rw_ln_b, dn_conv_w, dn_a_log, dn_dt_bias, dn_norm_w, gate_b,
              w_branch, w_out, norm2_g, moe_gr_w, moe_gr_b, moe_er_w, moe_er_b,
              moe_w1, moe_w3, moe_w2, final_g):
    for l in range(DEPTH):
        h = _rmsnorm(x, norm1_g[l])
        proj = jnp.einsum('bsd,dc->bsc', h, w_in[l])
        p_rw, p_dn, p_gate = _split(proj, (RW_COLS, DN_COLS, GATE_COLS))
        o_rw = _rwkv7_time_mix(p_rw, rw_mu[l], rw_w0[l], rw_w2[l], rw_a0[l], rw_a2[l], rw_g2[l],
                               rw_k_k[l], rw_k_a[l], rw_r_k[l], rw_ln_w[l], rw_ln_b[l])
        o_dn = _gated_deltanet(p_dn, dn_conv_w[l], dn_a_log[l], dn_dt_bias[l], dn_norm_w[l])
        y_rw = jnp.einsum('bsc,cd->bsd', o_rw, w_branch[l, :RW_WIDTH])
        y_dn = jnp.einsum('bsc,cd->bsd', o_dn, w_branch[l, RW_WIDTH:])
        g_rw, g_dn = _split(jax.nn.sigmoid(p_gate + gate_b[l]), (D_MODEL, D_MODEL))
        x = x + jnp.einsum('bsd,de->bse', g_rw * y_rw + g_dn * y_dn, w_out[l])
        x = x + _hier_moe(_rmsnorm(x, norm2_g[l]), moe_gr_w[l], moe_gr_b[l], moe_er_w[l],
                          moe_er_b[l], moe_w1[l], moe_w3[l], moe_w2[l])
    return _rmsnorm(x, final_g)
```

```python
import functools

import jax
import jax.numpy as jnp
from jax import lax
from jax.experimental import pallas as pl
from jax.experimental.pallas import tpu as pltpu

F32 = jnp.float32
BF16 = jnp.bfloat16

D_MODEL = 4096
RW_HEADS = 32
RW_HEAD = 64
RW_WIDTH = RW_HEADS * RW_HEAD
RW_DECAY_LORA = 96
RW_A_LORA = 96
RW_GATE_LORA = 256
RW_GN_EPS = 64e-5
RW_DECAY_SCALE = 0.606531
DN_HEADS = 16
DN_HEAD = 128
DN_KEY = DN_HEADS * DN_HEAD
DN_VAL = DN_HEADS * DN_HEAD
DN_QKV = 2 * DN_KEY + DN_VAL
DN_CONV = 4
N_GROUPS = 8
EXPERTS_PER_GROUP = 8
N_EXPERTS = N_GROUPS * EXPERTS_PER_GROUP
TOP_K = 2
D_EXPERT = 768
NORM_EPS = 1e-6

LANES = 128
HALO_ROWS = 8
CHUNK = 64
LORA_PAD = 128
RW_PCOLS = 3 * RW_WIDTH + 2 * LORA_PAD + RW_GATE_LORA
DN_PCOLS = DN_QKV + LORA_PAD + DN_VAL
RW_OFF_R = 0
RW_OFF_WD = RW_WIDTH
RW_OFF_K = RW_OFF_WD + LORA_PAD
RW_OFF_V = RW_OFF_K + RW_WIDTH
RW_OFF_AD = RW_OFF_V + RW_WIDTH
RW_OFF_GD = RW_OFF_AD + LORA_PAD
DN_OFF_SMALL = DN_QKV
DN_OFF_Z = DN_QKV + LORA_PAD
MOE_ROWS = 256
VMEM_LIMIT = 56 * 1024 * 1024


def _params(n_axes):
    return pltpu.CompilerParams(dimension_semantics=("arbitrary",) * n_axes,
                                vmem_limit_bytes=VMEM_LIMIT)


def _dot(a, b):
    return jnp.dot(a.astype(BF16), b.astype(BF16), preferred_element_type=F32)


def _dot_nt(a, b):
    return lax.dot_general(a.astype(BF16), b.astype(BF16), (((1,), (1,)), ((), ())),
                           preferred_element_type=F32)


def _split2(x):
    hi = x.astype(BF16)
    lo = (x - hi.astype(F32)).astype(BF16)
    return hi, lo


def _sigmoid(x):
    return 1.0 / (1.0 + jnp.exp(-x))


def _silu(x):
    return x * _sigmoid(x)


def _rmsnorm_kernel(x_ref, g_ref, o_ref):
    x = x_ref[...]
    ms = jnp.mean(x * x, axis=-1, keepdims=True)
    o_ref[...] = (x * lax.rsqrt(ms + NORM_EPS) * g_ref[...]).astype(o_ref.dtype)


def _rmsnorm(x, g, out_dtype, tm=256):
    t, d = x.shape
    tm = min(tm, t)
    return pl.pallas_call(
        _rmsnorm_kernel,
        out_shape=jax.ShapeDtypeStruct((t, d), out_dtype),
        grid=(t // tm,),
        in_specs=[pl.BlockSpec((tm, d), lambda i: (i, 0)), pl.BlockSpec((1, d), lambda i: (0, 0))],
        out_specs=pl.BlockSpec((tm, d), lambda i: (i, 0)),
        compiler_params=_params(1),
        name="rmsnorm",
    )(x, g.reshape(1, d))


def _mm_kernel(a_ref, b_ref, o_ref):
    o_ref[...] = jnp.dot(a_ref[...], b_ref[...], preferred_element_type=F32).astype(o_ref.dtype)


def _mm_gate_kernel(a_ref, b_ref, bias_ref, o_ref):
    acc = jnp.dot(a_ref[...], b_ref[...], preferred_element_type=F32)
    o_ref[...] = _sigmoid(acc + bias_ref[...]).astype(o_ref.dtype)


def _mm_res_kernel(a_ref, b_ref, x_ref, o_ref):
    o_ref[...] = x_ref[...] + jnp.dot(a_ref[...], b_ref[...], preferred_element_type=F32)


def _matmul(a, b, out_dtype, tm, tn, name, bias=None, residual=None):
    m, k = a.shape
    n = b.shape[1]
    tm = min(tm, m)
    in_specs = [pl.BlockSpec((tm, k), lambda i, j: (i, 0)), pl.BlockSpec((k, tn), lambda i, j: (0, j))]
    args = [a, b]
    kern = _mm_kernel
    if bias is not None:
        in_specs.append(pl.BlockSpec((1, tn), lambda i, j: (0, j)))
        args.append(bias.reshape(1, n))
        kern = _mm_gate_kernel
    if residual is not None:
        in_specs.append(pl.BlockSpec((tm, tn), lambda i, j: (i, j)))
        args.append(residual)
        kern = _mm_res_kernel
    return pl.pallas_call(
        kern,
        out_shape=jax.ShapeDtypeStruct((m, n), out_dtype),
        grid=(m // tm, n // tn),
        in_specs=in_specs,
        out_specs=pl.BlockSpec((tm, tn), lambda i, j: (i, j)),
        compiler_params=_params(2),
        name=name,
    )(*args)


def _mix_kernel(a1_ref, b1_ref, a2_ref, b2_ref, g1_ref, g2_ref, o_ref):
    y1 = jnp.dot(a1_ref[...], b1_ref[...], preferred_element_type=F32)
    y2 = jnp.dot(a2_ref[...], b2_ref[...], preferred_element_type=F32)
    o_ref[...] = (g1_ref[...].astype(F32) * y1 + g2_ref[...].astype(F32) * y2).astype(o_ref.dtype)


def _branch_mix(o_rw, o_dn, wb_rw, wb_dn, gates, tm=1024, tn=512):
    m, k = o_rw.shape
    n = wb_rw.shape[1]
    tm = min(tm, m)
    goff = n // tn
    return pl.pallas_call(
        _mix_kernel,
        out_shape=jax.ShapeDtypeStruct((m, n), BF16),
        grid=(m // tm, n // tn),
        in_specs=[pl.BlockSpec((tm, k), lambda i, j: (i, 0)), pl.BlockSpec((k, tn), lambda i, j: (0, j)),
                  pl.BlockSpec((tm, k), lambda i, j: (i, 0)), pl.BlockSpec((k, tn), lambda i, j: (0, j)),
                  pl.BlockSpec((tm, tn), lambda i, j: (i, j)),
                  pl.BlockSpec((tm, tn), lambda i, j: (i, j + goff))],
        out_specs=pl.BlockSpec((tm, tn), lambda i, j: (i, j)),
        compiler_params=_params(2),
        name="branch_mix",
    )(o_rw, wb_rw, o_dn, wb_dn, gates, gates)


def _tri_incl():
    r = lax.broadcasted_iota(jnp.int32, (CHUNK, CHUNK), 0)
    c = lax.broadcasted_iota(jnp.int32, (CHUNK, CHUNK), 1)
    return (r >= c).astype(BF16)


def _eye_bf16(n):
    r = lax.broadcasted_iota(jnp.int32, (n, n), 0)
    c = lax.broadcasted_iota(jnp.int32, (n, n), 1)
    return (r == c).astype(BF16)


def _cumsum_rows(x):
    tri = _tri_incl()
    hi, lo = _split2(x)
    return (jnp.dot(tri, hi, preferred_element_type=F32) + jnp.dot(tri, lo, preferred_element_type=F32))


def _unit_lower_inverse(lm):
    r = lax.broadcasted_iota(jnp.int32, (CHUNK, CHUNK), 0)
    c = lax.broadcasted_iota(jnp.int32, (CHUNK, CHUNK), 1)
    m = -lm
    p = jnp.where(r == c, 1.0, 0.0) + m
    for _ in range(5):
        m = _dot(m, m)
        p = p + _dot(p, m)
    return p


def _gdn_kernel(p_ref, halo_ref, cw_ref, alog_ref, dtb_ref, nw_ref, o_ref, ext_ref, state_ref):
    s = pl.program_id(1)

    @pl.when(s == 0)
    def _():
        state_ref[...] = jnp.zeros_like(state_ref)

    ext_ref[0:HALO_ROWS, :] = jnp.where(s == 0, 0.0, halo_ref[:, 0:DN_QKV])
    ext_ref[HALO_ROWS:HALO_ROWS + CHUNK, :] = p_ref[:, 0:DN_QKV]

    def conv_silu(col):
        cs = slice(col, col + DN_HEAD)
        acc = ext_ref[HALO_ROWS:HALO_ROWS + CHUNK, cs] * cw_ref[DN_CONV - 1:DN_CONV, cs]
        for j in range(DN_CONV - 1):
            r0 = HALO_ROWS - (DN_CONV - 1) + j
            acc = acc + ext_ref[r0:r0 + CHUNK, cs] * cw_ref[j:j + 1, cs]
        return _silu(acc)

    def l2norm(t):
        return t * lax.rsqrt(jnp.sum(t * t, axis=-1, keepdims=True) + NORM_EPS)

    small = p_ref[:, DN_OFF_SMALL:DN_OFF_SMALL + LORA_PAD]
    beta_all = _sigmoid(small)
    sp_in = small + dtb_ref[...]
    softplus = jnp.maximum(sp_in, 0.0) + jnp.log1p(jnp.exp(-jnp.abs(sp_in)))
    g_all = -jnp.exp(alog_ref[...]) * softplus
    gcum = _cumsum_rows(g_all)
    eye = _eye_bf16(LANES)
    ghi, glo = _split2(gcum)
    gcum_t = (lax.dot_general(eye, ghi, (((1,), (1,)), ((), ())), preferred_element_type=F32)
              + lax.dot_general(eye, glo, (((1,), (1,)), ((), ())), preferred_element_type=F32))

    r = lax.broadcasted_iota(jnp.int32, (CHUNK, CHUNK), 0)
    c = lax.broadcasted_iota(jnp.int32, (CHUNK, CHUNK), 1)
    causal = r >= c
    strict = r > c
    nw = nw_ref[...]

    for h in range(DN_HEADS):
        q = l2norm(conv_silu(h * DN_HEAD)) * (DN_HEAD ** -0.5)
        k = l2norm(conv_silu(DN_KEY + h * DN_HEAD))
        v = conv_silu(2 * DN_KEY + h * DN_HEAD)
        gcol = gcum[:, DN_HEADS + h:DN_HEADS + h + 1]
        grow = gcum_t[DN_HEADS + h:DN_HEADS + h + 1, :]
        decay = jnp.where(causal, jnp.exp(jnp.where(causal, gcol - grow, 0.0)), 0.0)
        beta = beta_all[:, h:h + 1]
        kb = k * beta
        kk = _dot_nt(jnp.concatenate([kb, q], axis=0), k)
        lmat = jnp.where(strict, kk[:CHUNK] * decay, 0.0)
        attn = kk[CHUNK:] * decay
        inv = _unit_lower_inverse(lmat)
        eg = jnp.exp(gcol)
        sol = _dot(inv, jnp.concatenate([v * beta, kb * eg], axis=1))
        u = sol[:, :DN_HEAD]
        w = sol[:, DN_HEAD:]
        glast = gcol[CHUNK - 1:CHUNK, :]
        k_tail = k * jnp.exp(glast - gcol)
        st = state_ref[h]
        wq = _dot(jnp.concatenate([w, q * eg], axis=0), st)
        v_new = u - wq[:CHUNK]
        o = wq[CHUNK:] + _dot(attn, v_new)
        k_tail_t = _dot_nt(eye, k_tail)
        state_ref[h] = st * jnp.exp(glast) + _dot(k_tail_t, v_new)
        o = o * lax.rsqrt(jnp.mean(o * o, axis=-1, keepdims=True) + NORM_EPS) * nw
        z = p_ref[:, DN_OFF_Z + h * DN_HEAD:DN_OFF_Z + (h + 1) * DN_HEAD]
        o_ref[:, h * DN_HEAD:(h + 1) * DN_HEAD] = (o * _silu(z)).astype(o_ref.dtype)


def _gated_deltanet(p_dn, batch, seq, conv_w, a_log, dt_bias, norm_w):
    t = batch * seq
    ns = seq // CHUNK
    hb = CHUNK // HALO_ROWS
    alog = jnp.zeros((1, LORA_PAD), F32).at[0, DN_HEADS:2 * DN_HEADS].set(a_log)
    dtb = jnp.zeros((1, LORA_PAD), F32).at[0, DN_HEADS:2 * DN_HEADS].set(dt_bias)
    return pl.pallas_call(
        _gdn_kernel,
        out_shape=jax.ShapeDtypeStruct((t, DN_VAL), BF16),
        grid=(batch, ns),
        in_specs=[pl.BlockSpec((CHUNK, DN_PCOLS), lambda b, s: (b * ns + s, 0)),
                  pl.BlockSpec((HALO_ROWS, DN_PCOLS), lambda b, s: (jnp.maximum((b * ns + s) * hb - 1, 0), 0)),
                  pl.BlockSpec((DN_CONV, DN_QKV), lambda b, s: (0, 0)),
                  pl.BlockSpec((1, LORA_PAD), lambda b, s: (0, 0)),
                  pl.BlockSpec((1, LORA_PAD), lambda b, s: (0, 0)),
                  pl.BlockSpec((1, DN_HEAD), lambda b, s: (0, 0))],
        out_specs=pl.BlockSpec((CHUNK, DN_VAL), lambda b, s: (b * ns + s, 0)),
        scratch_shapes=[pltpu.VMEM((HALO_ROWS + CHUNK, DN_QKV), F32),
                        pltpu.VMEM((DN_HEADS, DN_HEAD, DN_HEAD), F32)],
        compiler_params=_params(2),
        name="gated_deltanet",
    )(p_dn, p_dn, conv_w, alog, dtb, norm_w.reshape(1, DN_HEAD))


def _rwkv_kernel(p_ref, halo_ref, mu_ref, w0_ref, a0_ref, kk_ref, ka_ref, rk_ref, lnw_ref, lnb_ref,
                 w2_ref, a2_ref, g2_ref, o_ref, state_ref):
    s = pl.program_id(1)

    @pl.when(s == 0)
    def _():
        state_ref[...] = jnp.zeros_like(state_ref)

    row1 = lax.broadcasted_iota(jnp.int32, (CHUNK, 1), 0)

    def shifted(lo, hi):
        cur = p_ref[:, lo:hi]
        last = jnp.where(s == 0, 0.0, halo_ref[HALO_ROWS - 1:HALO_ROWS, lo:hi])
        prev = jnp.where(row1 == 0, last, pltpu.roll(cur, 1, 0))
        return cur + (prev - cur) * mu_ref[:, lo:hi]

    wd = shifted(RW_OFF_WD, RW_OFF_WD + LORA_PAD)
    ad = shifted(RW_OFF_AD, RW_OFF_AD + LORA_PAD)
    gd = shifted(RW_OFF_GD, RW_OFF_GD + RW_GATE_LORA)
    log_w = -RW_DECAY_SCALE * _sigmoid(w0_ref[...] + _dot(jnp.tanh(wd), w2_ref[...]))
    a_all = _sigmoid(a0_ref[...] + _dot(ad, a2_ref[...]))
    g_all = _dot(_sigmoid(gd), g2_ref[...])

    lane = lax.broadcasted_iota(jnp.int32, (CHUNK, LANES), 1)
    row = lax.broadcasted_iota(jnp.int32, (CHUNK, LANES), 0)
    tcol = lane % RW_HEAD
    strict_ss = row > tcol
    incl_ss = row >= tcol
    eye_ss = jnp.where(row == tcol, 1.0, 0.0)
    r2 = lax.broadcasted_iota(jnp.int32, (LANES, LANES), 0)
    c2 = lax.broadcasted_iota(jnp.int32, (LANES, LANES), 1)
    bd_mask = (r2 // RW_HEAD) == (c2 // RW_HEAD)
    ones_bd = bd_mask.astype(BF16)
    head0 = lane < RW_HEAD
    eye = _eye_bf16(LANES)

    def seg_sum(x):
        hi, lo = _split2(x)
        return (jnp.dot(hi, ones_bd, preferred_element_type=F32)
                + jnp.dot(lo, ones_bd, preferred_element_type=F32))

    def block_diag(x):
        return jnp.where(bd_mask, jnp.concatenate([x, x], axis=0), 0.0)

    def per_head_rows(x):
        return jnp.concatenate([jnp.where(head0, x, 0.0), jnp.where(head0, 0.0, x)], axis=0)

    for p in range(RW_HEADS // 2):
        sl = slice(p * LANES, (p + 1) * LANES)
        r = shifted(RW_OFF_R + p * LANES, RW_OFF_R + (p + 1) * LANES)
        k = shifted(RW_OFF_K + p * LANES, RW_OFF_K + (p + 1) * LANES)
        v = shifted(RW_OFF_V + p * LANES, RW_OFF_V + (p + 1) * LANES)
        lw = log_w[:, sl]
        a = a_all[:, sl]
        kkr = k * kk_ref[:, sl]
        kk = kkr * lax.rsqrt(seg_sum(kkr * kkr) + NORM_EPS)
        kh = k * (1.0 + (a - 1.0) * ka_ref[:, sl])
        b = kk * a
        gc = _cumsum_rows(lw)
        glast = gc[CHUNK - 1:CHUNK, :]
        e_in = jnp.exp(gc)
        e_ex = jnp.exp(gc - lw)
        e_neg = jnp.exp(-gc)
        e_tail = jnp.exp(glast - gc)
        x1 = jnp.concatenate([kk * e_ex, r * e_in], axis=0)
        ak = _dot_nt(x1, per_head_rows(kh * e_neg))
        ab = _dot_nt(x1, per_head_rows(b * e_neg))
        a_kv = jnp.where(strict_ss, ak[:CHUNK], 0.0)
        a_rk = jnp.where(incl_ss, ak[CHUNK:], 0.0)
        a_kb = jnp.where(strict_ss, ab[:CHUNK], 0.0)
        a_rb = jnp.where(incl_ss, ab[CHUNK:], 0.0)
        st = state_ref[p]
        xh = _dot_nt(x1, st)
        m = -a_kb
        inv = eye_ss + m
        for _ in range(5):
            m = _dot(m, block_diag(m))
            inv = inv + _dot(inv, block_diag(m))
        bd_v = block_diag(v)
        u = _dot(inv, block_diag(xh[:CHUNK] + _dot(a_kv, bd_v)))
        o = xh[CHUNK:] + _dot(jnp.concatenate([a_rk, a_rb], axis=1),
                              jnp.concatenate([bd_v, -block_diag(u)], axis=0))
        vu_t = _dot_nt(eye, jnp.concatenate([v, -u], axis=0))
        x3 = jnp.concatenate([kh * e_tail, b * e_tail], axis=0)
        state_ref[p] = st * jnp.exp(glast) + jnp.where(bd_mask, _dot(vu_t, x3), 0.0)
        mean = seg_sum(o) * (1.0 / RW_HEAD)
        oc = o - mean
        var = seg_sum(oc * oc) * (1.0 / RW_HEAD)
        on = oc * lax.rsqrt(var + RW_GN_EPS) * lnw_ref[:, sl] + lnb_ref[:, sl]
        bonus = seg_sum(r * kh * rk_ref[:, sl]) * v
        o_ref[:, sl] = ((on + bonus) * g_all[:, sl]).astype(o_ref.dtype)


def _rwkv7(p_rw, batch, seq, mu, w0, w2, a0, a2, g2, k_k, k_a, r_k, ln_w, ln_b):
    t = batch * seq
    ns = seq // CHUNK
    hb = CHUNK // HALO_ROWS
    vec = lambda x: x.reshape(1, RW_WIDTH)
    cvec = pl.BlockSpec((1, RW_WIDTH), lambda b, s: (0, 0))
    return pl.pallas_call(
        _rwkv_kernel,
        out_shape=jax.ShapeDtypeStruct((t, RW_WIDTH), BF16),
        grid=(batch, ns),
        in_specs=[pl.BlockSpec((CHUNK, RW_PCOLS), lambda b, s: (b * ns + s, 0)),
                  pl.BlockSpec((HALO_ROWS, RW_PCOLS), lambda b, s: (jnp.maximum((b * ns + s) * hb - 1, 0), 0)),
                  pl.BlockSpec((1, RW_PCOLS), lambda b, s: (0, 0)),
                  cvec, cvec, cvec, cvec, cvec, cvec, cvec,
                  pl.BlockSpec((LORA_PAD, RW_WIDTH), lambda b, s: (0, 0)),
                  pl.BlockSpec((LORA_PAD, RW_WIDTH), lambda b, s: (0, 0)),
                  pl.BlockSpec((RW_GATE_LORA, RW_WIDTH), lambda b, s: (0, 0))],
        out_specs=pl.BlockSpec((CHUNK, RW_WIDTH), lambda b, s: (b * ns + s, 0)),
        scratch_shapes=[pltpu.VMEM((RW_HEADS // 2, LANES, LANES), F32)],
        compiler_params=_params(2),
        name="rwkv7",
    )(p_rw, p_rw, mu.reshape(1, RW_PCOLS), vec(w0), vec(a0), vec(k_k), vec(k_a), vec(r_k), vec(ln_w),
      vec(ln_b), w2, a2, g2)


def _router_kernel(x_ref, g_ref, wr_ref, br_ref, xn_ref, eid_ref, wgt_ref):
    x = x_ref[...]
    xn = x * lax.rsqrt(jnp.mean(x * x, axis=-1, keepdims=True) + NORM_EPS) * g_ref[...]
    xn_ref[...] = xn
    xh, xl = _split2(xn)
    wh, wl = _split2(wr_ref[...])
    logits = (jnp.dot(xh, wh, preferred_element_type=F32) + jnp.dot(xh, wl, preferred_element_type=F32)
              + jnp.dot(xl, wh, preferred_element_type=F32)) + br_ref[...]
    tm = logits.shape[0]
    lane = lax.broadcasted_iota(jnp.int32, (tm, LANES), 1)
    neg = -1e30
    big = 4 * LANES

    def first_argmax(vals):
        mx = jnp.max(vals, axis=-1, keepdims=True)
        idx = jnp.min(jnp.where(vals == mx, lane, big), axis=-1, keepdims=True)
        return mx, idx

    gl = jnp.where(lane < N_GROUPS, logits, neg)
    gmax, gi = first_argmax(gl)
    gp = 1.0 / jnp.sum(jnp.where(lane < N_GROUPS, jnp.exp(gl - gmax), 0.0), axis=-1, keepdims=True)
    e_lo = N_GROUPS + gi * EXPERTS_PER_GROUP
    el = jnp.where((lane >= e_lo) & (lane < e_lo + EXPERTS_PER_GROUP), logits, neg)
    m1, i1 = first_argmax(el)
    m2, i2 = first_argmax(jnp.where(lane == i1, neg, el))
    e2 = jnp.exp(m2 - m1)
    w1 = gp / (1.0 + e2)
    w2 = gp * e2 / (1.0 + e2)
    eid_ref[...] = jnp.where(lane == 0, i1 - N_GROUPS, jnp.where(lane == 1, i2 - N_GROUPS, 0))
    wgt_ref[...] = jnp.where(lane == 0, w1, jnp.where(lane == 1, w2, 0.0))


def _router(x_mid, norm_g, wr, br, tm=256):
    t, d = x_mid.shape
    tm = min(tm, t)
    return pl.pallas_call(
        _router_kernel,
        out_shape=(jax.ShapeDtypeStruct((t, d), F32),
                   jax.ShapeDtypeStruct((t, LANES), jnp.int32),
                   jax.ShapeDtypeStruct((t, LANES), F32)),
        grid=(t // tm,),
        in_specs=[pl.BlockSpec((tm, d), lambda i: (i, 0)), pl.BlockSpec((1, d), lambda i: (0, 0)),
                  pl.BlockSpec((d, LANES), lambda i: (0, 0)), pl.BlockSpec((1, LANES), lambda i: (0, 0))],
        out_specs=(pl.BlockSpec((tm, d), lambda i: (i, 0)), pl.BlockSpec((tm, LANES), lambda i: (i, 0)),
                   pl.BlockSpec((tm, LANES), lambda i: (i, 0))),
        compiler_params=_params(1),
        name="router",
    )(x_mid, norm_g.reshape(1, d), wr, br)


def _row_copy_in(x_hbm, xbuf, sem, tok, slot, r):
    return pltpu.make_async_copy(x_hbm.at[pl.ds(tok, 1)], xbuf.at[slot, pl.ds(r, 1)], sem.at[slot])


def _moe_up_kernel(be_ref, bv_ref, tok_ref, x_hbm, w1_ref, w3_ref, h_ref, xbuf, sem):
    del be_ref
    i = pl.program_id(0)
    nb = pl.num_programs(0)
    slot = i % 2

    def gather(blk, sl, wait):
        def body(r, carry):
            cp = _row_copy_in(x_hbm, xbuf, sem, tok_ref[blk * MOE_ROWS + r], sl, r)
            if wait:
                cp.wait()
            else:
                cp.start()
            return carry
        lax.fori_loop(0, MOE_ROWS, body, 0)

    @pl.when(jnp.logical_and(i == 0, bv_ref[0] == 1))
    def _():
        gather(0, 0, False)

    nxt = jnp.minimum(i + 1, nb - 1)

    @pl.when(jnp.logical_and(i + 1 < nb, bv_ref[nxt] == 1))
    def _():
        gather(nxt, 1 - slot, False)

    @pl.when(bv_ref[i] == 1)
    def _():
        gather(i, slot, True)
        x = xbuf[slot].astype(BF16)
        h1 = jnp.dot(x, w1_ref[0], preferred_element_type=F32)
        h3 = jnp.dot(x, w3_ref[0], preferred_element_type=F32)
        h_ref[...] = (_silu(h1) * h3).astype(h_ref.dtype)

    @pl.when(bv_ref[i] == 0)
    def _():
        h_ref[...] = jnp.zeros_like(h_ref)


def _moe_down_kernel(be_ref, bv_ref, dst_ref, h_ref, w2_ref, y_hbm, ybuf, sem):
    del be_ref
    i = pl.program_id(0)
    nb = pl.num_programs(0)
    slot = i % 2

    def scatter(blk, sl, wait):
        def body(r, carry):
            d = dst_ref[blk * MOE_ROWS + r]

            @pl.when(d >= 0)
            def _():
                cp = pltpu.make_async_copy(ybuf.at[sl, pl.ds(r, 1)], y_hbm.at[pl.ds(d, 1)], sem.at[sl])
                if wait:
                    cp.wait()
                else:
                    cp.start()
            return carry
        lax.fori_loop(0, MOE_ROWS, body, 0)

    prev2 = jnp.maximum(i - 2, 0)

    @pl.when(jnp.logical_and(i >= 2, bv_ref[prev2] == 1))
    def _():
        scatter(prev2, slot, True)

    @pl.when(bv_ref[i] == 1)
    def _():
        ybuf[slot] = jnp.dot(h_ref[...], w2_ref[0], preferred_element_type=F32)
        scatter(i, slot, False)

    prev1 = jnp.maximum(i - 1, 0)

    @pl.when(jnp.logical_and(i == nb - 1, jnp.logical_and(i >= 1, bv_ref[prev1] == 1)))
    def _():
        scatter(prev1, 1 - slot, True)

    @pl.when(jnp.logical_and(i == nb - 1, bv_ref[i] == 1))
    def _():
        scatter(i, slot, True)


def _moe_experts(xn, eid, w1, w3, w2):
    t, d = xn.shape
    a = t * TOP_K
    e_flat = eid.reshape(a)
    tok_flat = jnp.repeat(jnp.arange(t, dtype=jnp.int32), TOP_K)
    dst_flat = (jnp.arange(a, dtype=jnp.int32) % TOP_K) * t + tok_flat
    onehot = (e_flat[:, None] == jnp.arange(N_EXPERTS, dtype=jnp.int32)[None, :]).astype(jnp.int32)
    csum = jnp.cumsum(onehot, axis=0)
    rank = jnp.sum(csum * onehot, axis=1) - 1
    counts = csum[-1]
    pcounts = (counts + MOE_ROWS - 1) // MOE_ROWS * MOE_ROWS
    pend = jnp.cumsum(pcounts)
    poffs = pend - pcounts
    dest = poffs[e_flat] + rank
    p_rows = a + N_EXPERTS * MOE_ROWS
    nb = p_rows // MOE_ROWS
    slot_tok = jnp.zeros((p_rows,), jnp.int32).at[dest].set(tok_flat)
    slot_dst = jnp.full((p_rows,), -1, jnp.int32).at[dest].set(dst_flat)
    starts = jnp.arange(nb, dtype=jnp.int32) * MOE_ROWS
    block_e = jnp.minimum(jnp.searchsorted(pend, starts, side='right'), N_EXPERTS - 1).astype(jnp.int32)
    block_v = (starts < pend[-1]).astype(jnp.int32)
    last_e = block_e[jnp.maximum(jnp.sum(block_v) - 1, 0)]
    block_e = jnp.where(block_v == 1, block_e, last_e)

    h = pl.pallas_call(
        _moe_up_kernel,
        out_shape=jax.ShapeDtypeStruct((p_rows, D_EXPERT), BF16),
        grid_spec=pltpu.PrefetchScalarGridSpec(
            num_scalar_prefetch=3,
            grid=(nb,),
            in_specs=[pl.BlockSpec(memory_space=pl.ANY),
                      pl.BlockSpec((1, d, D_EXPERT), lambda i, be, bv, tk: (be[i], 0, 0)),
                      pl.BlockSpec((1, d, D_EXPERT), lambda i, be, bv, tk: (be[i], 0, 0))],
            out_specs=pl.BlockSpec((MOE_ROWS, D_EXPERT), lambda i, be, bv, tk: (i, 0)),
            scratch_shapes=[pltpu.VMEM((2, MOE_ROWS, d), F32), pltpu.SemaphoreType.DMA((2,))]),
        compiler_params=_params(1),
        name="moe_up",
    )(block_e, block_v, slot_tok, xn, w1, w3)

    y = pl.pallas_call(
        _moe_down_kernel,
        out_shape=jax.ShapeDtypeStruct((TOP_K * t, d), F32),
        grid_spec=pltpu.PrefetchScalarGridSpec(
            num_scalar_prefetch=3,
            grid=(nb,),
            in_specs=[pl.BlockSpec((MOE_ROWS, D_EXPERT), lambda i, be, bv, ds: (i, 0)),
                      pl.BlockSpec((1, D_EXPERT, d), lambda i, be, bv, ds: (be[i], 0, 0))],
            out_specs=pl.BlockSpec(memory_space=pl.ANY),
            scratch_shapes=[pltpu.VMEM((2, MOE_ROWS, d), F32), pltpu.SemaphoreType.DMA((2,))]),
        compiler_params=_params(1),
        name="moe_down",
    )(block_e, block_v, slot_dst, h, w2)
    return y


def _final_kernel(x_ref, y0_ref, y1_ref, wgt_ref, g_ref, o_ref):
    wgt = wgt_ref[...]
    x = x_ref[...] + wgt[:, 0:1] * y0_ref[...] + wgt[:, 1:2] * y1_ref[...]
    o_ref[...] = x * lax.rsqrt(jnp.mean(x * x, axis=-1, keepdims=True) + NORM_EPS) * g_ref[...]


def _final(x_mid, y, wgt, g, tm=256):
    t, d = x_mid.shape
    tm = min(tm, t)
    nblk = t // tm
    return pl.pallas_call(
        _final_kernel,
        out_shape=jax.ShapeDtypeStruct((t, d), F32),
        grid=(nblk,),
        in_specs=[pl.BlockSpec((tm, d), lambda i: (i, 0)), pl.BlockSpec((tm, d), lambda i: (i, 0)),
                  pl.BlockSpec((tm, d), lambda i: (i + nblk, 0)), pl.BlockSpec((tm, LANES), lambda i: (i, 0)),
                  pl.BlockSpec((1, d), lambda i: (0, 0))],
        out_specs=pl.BlockSpec((tm, d), lambda i: (i, 0)),
        compiler_params=_params(1),
        name="final_combine",
    )(x_mid, y, y, wgt, g.reshape(1, d))


def _pad_cols(w, n):
    return jnp.pad(w, ((0, 0), (0, n - w.shape[1])))


def _pad_rows(w, n):
    return jnp.pad(w, ((0, n - w.shape[0]), (0, 0)))


def _layer(x2, batch, seq, norm1_g, w_in, rw_mu, rw_w0, rw_w2, rw_a0, rw_a2, rw_g2, rw_k_k, rw_k_a, rw_r_k,
           rw_ln_w, rw_ln_b, dn_conv_w, dn_a_log, dn_dt_bias, dn_norm_w, gate_b, w_branch, w_out, norm2_g,
           moe_gr_w, moe_gr_b, moe_er_w, moe_er_b, moe_w1, moe_w3, moe_w2):
    sizes = (RW_WIDTH, RW_DECAY_LORA, RW_WIDTH, RW_WIDTH, RW_A_LORA, RW_GATE_LORA,
             DN_QKV, 2 * DN_HEADS, DN_VAL, 2 * D_MODEL)
    offs = [0]
    for sz in sizes:
        offs.append(offs[-1] + sz)
    cols = lambda w, i: w[..., offs[i]:offs[i + 1]]
    w_rw = jnp.concatenate([cols(w_in, 0), _pad_cols(cols(w_in, 1), LORA_PAD), cols(w_in, 2), cols(w_in, 3),
                            _pad_cols(cols(w_in, 4), LORA_PAD), cols(w_in, 5)], axis=1).astype(BF16)
    w_dn = jnp.concatenate([cols(w_in, 6), _pad_cols(cols(w_in, 7), LORA_PAD), cols(w_in, 8)],
                           axis=1).astype(BF16)
    w_gate = cols(w_in, 9).astype(BF16)
    mu2 = rw_mu.reshape(1, -1)
    mu = jnp.concatenate([cols(mu2, 0), _pad_cols(cols(mu2, 1), LORA_PAD), cols(mu2, 2), cols(mu2, 3),
                          _pad_cols(cols(mu2, 4), LORA_PAD), cols(mu2, 5)], axis=1)

    h = _rmsnorm(x2, norm1_g, BF16)
    p_rw = _matmul(h, w_rw, F32, 1024, 512, "proj_rwkv")
    p_dn = _matmul(h, w_dn, F32, 1024, 640, "proj_gdn")
    gates = _matmul(h, w_gate, BF16, 1024, 512, "proj_gate", bias=gate_b)

    o_rw = _rwkv7(p_rw, batch, seq, mu, rw_w0, _pad_rows(rw_w2, LORA_PAD).astype(BF16), rw_a0,
                  _pad_rows(rw_a2, LORA_PAD).astype(BF16), rw_g2.astype(BF16), rw_k_k, rw_k_a,
                  rw_r_k.reshape(-1), rw_ln_w, rw_ln_b)
    o_dn = _gated_deltanet(p_dn, batch, seq, dn_conv_w, dn_a_log, dn_dt_bias, dn_norm_w)

    mix = _branch_mix(o_rw, o_dn, w_branch[:RW_WIDTH].astype(BF16), w_branch[RW_WIDTH:].astype(BF16), gates)
    x_mid = _matmul(mix, w_out.astype(BF16), F32, 1024, 512, "proj_out", residual=x2)

    wr = _pad_cols(jnp.concatenate([moe_gr_w, moe_er_w], axis=1), LANES)
    br = _pad_cols(jnp.concatenate([moe_gr_b, moe_er_b]).reshape(1, -1), LANES)
    xn, eid, wgt = _router(x_mid, norm2_g, wr, br)
    y = _moe_experts(xn, eid[:, :TOP_K], moe_w1.astype(BF16), moe_w3.astype(BF16), moe_w2.astype(BF16))
    return x_mid, y, wgt


def kernel(x, norm1_g, w_in, rw_mu, rw_w0, rw_w2, rw_a0, rw_a2, rw_g2, rw_k_k, rw_k_a, rw_r_k, rw_ln_w, rw_ln_b, dn_conv_w, dn_a_log, dn_dt_bias, dn_norm_w, gate_b, w_branch, w_out, norm2_g, moe_gr_w, moe_gr_b, moe_er_w, moe_er_b, moe_w1, moe_w3, moe_w2, final_g):
    batch, seq, d = x.shape
    assert norm1_g.shape[0] == 1, "single-layer block"
    x_mid, y, wgt = _layer(
        x.reshape(batch * seq, d), batch, seq, norm1_g[0], w_in[0], rw_mu[0], rw_w0[0], rw_w2[0], rw_a0[0],
        rw_a2[0], rw_g2[0], rw_k_k[0], rw_k_a[0], rw_r_k[0], rw_ln_w[0], rw_ln_b[0], dn_conv_w[0], dn_a_log[0],
        dn_dt_bias[0], dn_norm_w[0], gate_b[0], w_branch[0], w_out[0], norm2_g[0], moe_gr_w[0], moe_gr_b[0],
        moe_er_w[0], moe_er_b[0], moe_w1[0], moe_w3[0], moe_w2[0])
    return _final(x_mid, y, wgt, final_g).reshape(batch, seq, d)
```

```python
import functools

import jax
import jax.numpy as jnp
from jax import lax
from jax.experimental import pallas as pl
from jax.experimental.pallas import tpu as pltpu

F32 = jnp.float32
BF16 = jnp.bfloat16

D_MODEL = 4096
RW_HEADS = 32
RW_HEAD = 64
RW_WIDTH = RW_HEADS * RW_HEAD
RW_DECAY_LORA = 96
RW_A_LORA = 96
RW_GATE_LORA = 256
RW_GN_EPS = 64e-5
RW_DECAY_SCALE = 0.606531
DN_HEADS = 16
DN_HEAD = 128
DN_KEY = DN_HEADS * DN_HEAD
DN_VAL = DN_HEADS * DN_HEAD
DN_QKV = 2 * DN_KEY + DN_VAL
DN_CONV = 4
N_GROUPS = 8
EXPERTS_PER_GROUP = 8
N_EXPERTS = N_GROUPS * EXPERTS_PER_GROUP
TOP_K = 2
D_EXPERT = 768
NORM_EPS = 1e-6

LANES = 128
HALO_ROWS = 8
CHUNK = 64
LORA_PAD = 128
RW_PCOLS = 3 * RW_WIDTH + 2 * LORA_PAD + RW_GATE_LORA
DN_PCOLS = DN_QKV + LORA_PAD + DN_VAL
RW_OFF_R = 0
RW_OFF_WD = RW_WIDTH
RW_OFF_K = RW_OFF_WD + LORA_PAD
RW_OFF_V = RW_OFF_K + RW_WIDTH
RW_OFF_AD = RW_OFF_V + RW_WIDTH
RW_OFF_GD = RW_OFF_AD + LORA_PAD
DN_OFF_SMALL = DN_QKV
DN_OFF_Z = DN_QKV + LORA_PAD
MOE_ROWS = 256
VMEM_LIMIT = 56 * 1024 * 1024


def _params(n_axes):
    return pltpu.CompilerParams(dimension_semantics=("arbitrary",) * n_axes,
                                vmem_limit_bytes=VMEM_LIMIT)


def _dot(a, b):
    return jnp.dot(a.astype(BF16), b.astype(BF16), preferred_element_type=F32)


def _dot_nt(a, b):
    return lax.dot_general(a.astype(BF16), b.astype(BF16), (((1,), (1,)), ((), ())),
                           preferred_element_type=F32)


def _split2(x):
    hi = x.astype(BF16)
    lo = (x - hi.astype(F32)).astype(BF16)
    return hi, lo


def _sigmoid(x):
    return 1.0 / (1.0 + jnp.exp(-x))


def _silu(x):
    return x * _sigmoid(x)


def _rmsnorm_kernel(x_ref, g_ref, o_ref):
    x = x_ref[...]
    ms = jnp.mean(x * x, axis=-1, keepdims=True)
    o_ref[...] = (x * lax.rsqrt(ms + NORM_EPS) * g_ref[...]).astype(o_ref.dtype)


def _rmsnorm(x, g, out_dtype, tm=256):
    t, d = x.shape
    tm = min(tm, t)
    return pl.pallas_call(
        _rmsnorm_kernel,
        out_shape=jax.ShapeDtypeStruct((t, d), out_dtype),
        grid=(t // tm,),
        in_specs=[pl.BlockSpec((tm, d), lambda i: (i, 0)), pl.BlockSpec((1, d), lambda i: (0, 0))],
        out_specs=pl.BlockSpec((tm, d), lambda i: (i, 0)),
        compiler_params=_params(1),
        name="rmsnorm",
    )(x, g.reshape(1, d))


def _mm_kernel(a_ref, b_ref, o_ref):
    o_ref[...] = jnp.dot(a_ref[...], b_ref[...], preferred_element_type=F32).astype(o_ref.dtype)


def _mm_gate_kernel(a_ref, b_ref, bias_ref, o_ref):
    acc = jnp.dot(a_ref[...], b_ref[...], preferred_element_type=F32)
    o_ref[...] = _sigmoid(acc + bias_ref[...]).astype(o_ref.dtype)


def _mm_res_kernel(a_ref, b_ref, x_ref, o_ref):
    o_ref[...] = x_ref[...] + jnp.dot(a_ref[...], b_ref[...], preferred_element_type=F32)


def _matmul(a, b, out_dtype, tm, tn, name, bias=None, residual=None):
    m, k = a.shape
    n = b.shape[1]
    tm = min(tm, m)
    in_specs = [pl.BlockSpec((tm, k), lambda i, j: (i, 0)), pl.BlockSpec((k, tn), lambda i, j: (0, j))]
    args = [a, b]
    kern = _mm_kernel
    if bias is not None:
        in_specs.append(pl.BlockSpec((1, tn), lambda i, j: (0, j)))
        args.append(bias.reshape(1, n))
        kern = _mm_gate_kernel
    if residual is not None:
        in_specs.append(pl.BlockSpec((tm, tn), lambda i, j: (i, j)))
        args.append(residual)
        kern = _mm_res_kernel
    return pl.pallas_call(
        kern,
        out_shape=jax.ShapeDtypeStruct((m, n), out_dtype),
        grid=(m // tm, n // tn),
        in_specs=in_specs,
        out_specs=pl.BlockSpec((tm, tn), lambda i, j: (i, j)),
        compiler_params=_params(2),
        name=name,
    )(*args)


def _mix_kernel(a1_ref, b1_ref, a2_ref, b2_ref, g1_ref, g2_ref, o_ref):
    y1 = jnp.dot(a1_ref[...], b1_ref[...], preferred_element_type=F32)
    y2 = jnp.dot(a2_ref[...], b2_ref[...], preferred_element_type=F32)
    o_ref[...] = (g1_ref[...].astype(F32) * y1 + g2_ref[...].astype(F32) * y2).astype(o_ref.dtype)


def _branch_mix(o_rw, o_dn, wb_rw, wb_dn, gates, tm=1024, tn=512):
    m, k = o_rw.shape
    n = wb_rw.shape[1]
    tm = min(tm, m)
    goff = n // tn
    return pl.pallas_call(
        _mix_kernel,
        out_shape=jax.ShapeDtypeStruct((m, n), BF16),
        grid=(m // tm, n // tn),
        in_specs=[pl.BlockSpec((tm, k), lambda i, j: (i, 0)), pl.BlockSpec((k, tn), lambda i, j: (0, j)),
                  pl.BlockSpec((tm, k), lambda i, j: (i, 0)), pl.BlockSpec((k, tn), lambda i, j: (0, j)),
                  pl.BlockSpec((tm, tn), lambda i, j: (i, j)),
                  pl.BlockSpec((tm, tn), lambda i, j: (i, j + goff))],
        out_specs=pl.BlockSpec((tm, tn), lambda i, j: (i, j)),
        compiler_params=_params(2),
        name="branch_mix",
    )(o_rw, wb_rw, o_dn, wb_dn, gates, gates)


def _tri_incl():
    r = lax.broadcasted_iota(jnp.int32, (CHUNK, CHUNK), 0)
    c = lax.broadcasted_iota(jnp.int32, (CHUNK, CHUNK), 1)
    return (r >= c).astype(BF16)


def _eye_bf16(n):
    r = lax.broadcasted_iota(jnp.int32, (n, n), 0)
    c = lax.broadcasted_iota(jnp.int32, (n, n), 1)
    return (r == c).astype(BF16)


def _cumsum_rows(x):
    tri = _tri_incl()
    hi, lo = _split2(x)
    return (jnp.dot(tri, hi, preferred_element_type=F32) + jnp.dot(tri, lo, preferred_element_type=F32))


def _gdn_kernel(p_ref, halo_ref, cw_ref, alog_ref, dtb_ref, nw_ref, o_ref, ext_ref, state_ref):
    s = pl.program_id(1)

    @pl.when(s == 0)
    def _():
        state_ref[...] = jnp.zeros_like(state_ref)

    ext_ref[0:HALO_ROWS, :] = jnp.where(s == 0, 0.0, halo_ref[:, 0:DN_QKV])
    ext_ref[HALO_ROWS:HALO_ROWS + CHUNK, :] = p_ref[:, 0:DN_QKV]

    def conv_silu(col):
        cs = slice(col, col + DN_HEAD)
        acc = ext_ref[HALO_ROWS:HALO_ROWS + CHUNK, cs] * cw_ref[DN_CONV - 1:DN_CONV, cs]
        for j in range(DN_CONV - 1):
            r0 = HALO_ROWS - (DN_CONV - 1) + j
            acc = acc + ext_ref[r0:r0 + CHUNK, cs] * cw_ref[j:j + 1, cs]
        return _silu(acc)

    def l2norm(t):
        return t * lax.rsqrt(jnp.sum(t * t, axis=-1, keepdims=True) + NORM_EPS)

    small = p_ref[:, DN_OFF_SMALL:DN_OFF_SMALL + LORA_PAD]
    beta_all = _sigmoid(small)
    sp_in = small + dtb_ref[...]
    softplus = jnp.maximum(sp_in, 0.0) + jnp.log1p(jnp.exp(-jnp.abs(sp_in)))
    g_all = -jnp.exp(alog_ref[...]) * softplus
    gcum = _cumsum_rows(g_all)
    eye = _eye_bf16(LANES)
    ghi, glo = _split2(gcum)
    gcum_t = (lax.dot_general(eye, ghi, (((1,), (1,)), ((), ())), preferred_element_type=F32)
              + lax.dot_general(eye, glo, (((1,), (1,)), ((), ())), preferred_element_type=F32))

    r = lax.broadcasted_iota(jnp.int32, (CHUNK, CHUNK), 0)
    c = lax.broadcasted_iota(jnp.int32, (CHUNK, CHUNK), 1)
    causal = r >= c
    strict = r > c
    eye_c = jnp.where(r == c, 1.0, 0.0)
    nw = nw_ref[...]
    heads = range(DN_HEADS)

    q = [l2norm(conv_silu(h * DN_HEAD)) * (DN_HEAD ** -0.5) for h in heads]
    k = [l2norm(conv_silu(DN_KEY + h * DN_HEAD)) for h in heads]
    v = [conv_silu(2 * DN_KEY + h * DN_HEAD) for h in heads]
    gcol = [gcum[:, DN_HEADS + h:DN_HEADS + h + 1] for h in heads]
    glast = [g[CHUNK - 1:CHUNK, :] for g in gcol]
    decay = [jnp.where(causal, jnp.exp(jnp.where(causal, gcol[h] - gcum_t[DN_HEADS + h:DN_HEADS + h + 1, :], 0.0)),
                       0.0) for h in heads]
    beta = [beta_all[:, h:h + 1] for h in heads]
    kb = [k[h] * beta[h] for h in heads]
    eg = [jnp.exp(g) for g in gcol]
    kk = [_dot_nt(jnp.concatenate([kb[h], q[h]], axis=0), k[h]) for h in heads]
    k_tail_t = [_dot_nt(eye, k[h] * jnp.exp(glast[h] - gcol[h])) for h in heads]
    attn = [kk[h][CHUNK:] * decay[h] for h in heads]
    m = [-jnp.where(strict, kk[h][:CHUNK] * decay[h], 0.0) for h in heads]
    inv = [eye_c + m[h] for h in heads]
    for _ in range(5):
        m = [_dot(m[h], m[h]) for h in heads]
        inv = [inv[h] + _dot(inv[h], m[h]) for h in heads]
    sol = [_dot(inv[h], jnp.concatenate([v[h] * beta[h], kb[h] * eg[h]], axis=1)) for h in heads]
    st = [state_ref[h] for h in heads]
    wq = [_dot(jnp.concatenate([sol[h][:, DN_HEAD:], q[h] * eg[h]], axis=0), st[h]) for h in heads]
    v_new = [sol[h][:, :DN_HEAD] - wq[h][:CHUNK] for h in heads]
    o = [wq[h][CHUNK:] + _dot(attn[h], v_new[h]) for h in heads]
    for h in heads:
        state_ref[h] = st[h] * jnp.exp(glast[h]) + _dot(k_tail_t[h], v_new[h])
    for h in heads:
        on = o[h] * lax.rsqrt(jnp.mean(o[h] * o[h], axis=-1, keepdims=True) + NORM_EPS) * nw
        z = p_ref[:, DN_OFF_Z + h * DN_HEAD:DN_OFF_Z + (h + 1) * DN_HEAD]
        o_ref[:, h * DN_HEAD:(h + 1) * DN_HEAD] = (on * _silu(z)).astype(o_ref.dtype)


def _gated_deltanet(p_dn, batch, seq, conv_w, a_log, dt_bias, norm_w):
    t = batch * seq
    ns = seq // CHUNK
    hb = CHUNK // HALO_ROWS
    alog = jnp.zeros((1, LORA_PAD), F32).at[0, DN_HEADS:2 * DN_HEADS].set(a_log)
    dtb = jnp.zeros((1, LORA_PAD), F32).at[0, DN_HEADS:2 * DN_HEADS].set(dt_bias)
    return pl.pallas_call(
        _gdn_kernel,
        out_shape=jax.ShapeDtypeStruct((t, DN_VAL), BF16),
        grid=(batch, ns),
        in_specs=[pl.BlockSpec((CHUNK, DN_PCOLS), lambda b, s: (b * ns + s, 0)),
                  pl.BlockSpec((HALO_ROWS, DN_PCOLS), lambda b, s: (jnp.maximum((b * ns + s) * hb - 1, 0), 0)),
                  pl.BlockSpec((DN_CONV, DN_QKV), lambda b, s: (0, 0)),
                  pl.BlockSpec((1, LORA_PAD), lambda b, s: (0, 0)),
                  pl.BlockSpec((1, LORA_PAD), lambda b, s: (0, 0)),
                  pl.BlockSpec((1, DN_HEAD), lambda b, s: (0, 0))],
        out_specs=pl.BlockSpec((CHUNK, DN_VAL), lambda b, s: (b * ns + s, 0)),
        scratch_shapes=[pltpu.VMEM((HALO_ROWS + CHUNK, DN_QKV), F32),
                        pltpu.VMEM((DN_HEADS, DN_HEAD, DN_HEAD), F32)],
        compiler_params=_params(2),
        name="gated_deltanet",
    )(p_dn, p_dn, conv_w, alog, dtb, norm_w.reshape(1, DN_HEAD))


def _rwkv_kernel(p_ref, halo_ref, mu_ref, w0_ref, a0_ref, kk_ref, ka_ref, rk_ref, lnw_ref, lnb_ref,
                 w2_ref, a2_ref, g2_ref, o_ref, state_ref):
    s = pl.program_id(1)

    @pl.when(s == 0)
    def _():
        state_ref[...] = jnp.zeros_like(state_ref)

    row1 = lax.broadcasted_iota(jnp.int32, (CHUNK, 1), 0)

    def shifted(lo, hi):
        cur = p_ref[:, lo:hi]
        last = jnp.where(s == 0, 0.0, halo_ref[HALO_ROWS - 1:HALO_ROWS, lo:hi])
        prev = jnp.where(row1 == 0, last, pltpu.roll(cur, 1, 0))
        return cur + (prev - cur) * mu_ref[:, lo:hi]

    wd = shifted(RW_OFF_WD, RW_OFF_WD + LORA_PAD)
    ad = shifted(RW_OFF_AD, RW_OFF_AD + LORA_PAD)
    gd = shifted(RW_OFF_GD, RW_OFF_GD + RW_GATE_LORA)
    log_w = -RW_DECAY_SCALE * _sigmoid(w0_ref[...] + _dot(jnp.tanh(wd), w2_ref[...]))
    a_all = _sigmoid(a0_ref[...] + _dot(ad, a2_ref[...]))
    g_all = _dot(_sigmoid(gd), g2_ref[...])

    lane = lax.broadcasted_iota(jnp.int32, (CHUNK, LANES), 1)
    row = lax.broadcasted_iota(jnp.int32, (CHUNK, LANES), 0)
    tcol = lane % RW_HEAD
    strict_ss = row > tcol
    incl_ss = row >= tcol
    eye_ss = jnp.where(row == tcol, 1.0, 0.0)
    r2 = lax.broadcasted_iota(jnp.int32, (LANES, LANES), 0)
    c2 = lax.broadcasted_iota(jnp.int32, (LANES, LANES), 1)
    bd_mask = (r2 // RW_HEAD) == (c2 // RW_HEAD)
    ones_bd = bd_mask.astype(BF16)
    head0 = lane < RW_HEAD
    eye = _eye_bf16(LANES)

    def seg_sum(x):
        hi, lo = _split2(x)
        return (jnp.dot(hi, ones_bd, preferred_element_type=F32)
                + jnp.dot(lo, ones_bd, preferred_element_type=F32))

    def block_diag(x):
        return jnp.where(bd_mask, jnp.concatenate([x, x], axis=0), 0.0)

    def per_head_rows(x):
        return jnp.concatenate([jnp.where(head0, x, 0.0), jnp.where(head0, 0.0, x)], axis=0)

    pairs = range(RW_HEADS // 2)
    sl = [slice(p * LANES, (p + 1) * LANES) for p in pairs]
    r = [shifted(RW_OFF_R + p * LANES, RW_OFF_R + (p + 1) * LANES) for p in pairs]
    k = [shifted(RW_OFF_K + p * LANES, RW_OFF_K + (p + 1) * LANES) for p in pairs]
    v = [shifted(RW_OFF_V + p * LANES, RW_OFF_V + (p + 1) * LANES) for p in pairs]
    lw = [log_w[:, sl[p]] for p in pairs]
    a = [a_all[:, sl[p]] for p in pairs]
    kkr = [k[p] * kk_ref[:, sl[p]] for p in pairs]
    kss = [seg_sum(kkr[p] * kkr[p]) for p in pairs]
    gc = [_cumsum_rows(lw[p]) for p in pairs]
    kk = [kkr[p] * lax.rsqrt(kss[p] + NORM_EPS) for p in pairs]
    kh = [k[p] * (1.0 + (a[p] - 1.0) * ka_ref[:, sl[p]]) for p in pairs]
    b = [kk[p] * a[p] for p in pairs]
    bonus_s = [seg_sum(r[p] * kh[p] * rk_ref[:, sl[p]]) for p in pairs]
    glast = [g[CHUNK - 1:CHUNK, :] for g in gc]
    e_neg = [jnp.exp(-g) for g in gc]
    e_tail = [jnp.exp(glast[p] - gc[p]) for p in pairs]
    x1 = [jnp.concatenate([kk[p] * jnp.exp(gc[p] - lw[p]), r[p] * jnp.exp(gc[p])], axis=0) for p in pairs]
    x3 = [jnp.concatenate([kh[p] * e_tail[p], b[p] * e_tail[p]], axis=0) for p in pairs]
    st = [state_ref[p] for p in pairs]
    ak = [_dot_nt(x1[p], per_head_rows(kh[p] * e_neg[p])) for p in pairs]
    ab = [_dot_nt(x1[p], per_head_rows(b[p] * e_neg[p])) for p in pairs]
    xh = [_dot_nt(x1[p], st[p]) for p in pairs]
    bd_v = [block_diag(v[p]) for p in pairs]
    rhs = [xh[p][:CHUNK] + _dot(jnp.where(strict_ss, ak[p][:CHUNK], 0.0), bd_v[p]) for p in pairs]
    m = [-jnp.where(strict_ss, ab[p][:CHUNK], 0.0) for p in pairs]
    inv = [eye_ss + m[p] for p in pairs]
    for _ in range(5):
        m = [_dot(m[p], block_diag(m[p])) for p in pairs]
        inv = [inv[p] + _dot(inv[p], block_diag(m[p])) for p in pairs]
    u = [_dot(inv[p], block_diag(rhs[p])) for p in pairs]
    o = [xh[p][CHUNK:] + _dot(jnp.concatenate([jnp.where(incl_ss, ak[p][CHUNK:], 0.0),
                                               jnp.where(incl_ss, ab[p][CHUNK:], 0.0)], axis=1),
                              jnp.concatenate([bd_v[p], -block_diag(u[p])], axis=0)) for p in pairs]
    vu_t = [_dot_nt(eye, jnp.concatenate([v[p], -u[p]], axis=0)) for p in pairs]
    for p in pairs:
        state_ref[p] = st[p] * jnp.exp(glast[p]) + jnp.where(bd_mask, _dot(vu_t[p], x3[p]), 0.0)
    oc = [o[p] - seg_sum(o[p]) * (1.0 / RW_HEAD) for p in pairs]
    var = [seg_sum(oc[p] * oc[p]) * (1.0 / RW_HEAD) for p in pairs]
    for p in pairs:
        on = oc[p] * lax.rsqrt(var[p] + RW_GN_EPS) * lnw_ref[:, sl[p]] + lnb_ref[:, sl[p]]
        o_ref[:, sl[p]] = ((on + bonus_s[p] * v[p]) * g_all[:, sl[p]]).astype(o_ref.dtype)


def _rwkv7(p_rw, batch, seq, mu, w0, w2, a0, a2, g2, k_k, k_a, r_k, ln_w, ln_b):
    t = batch * seq
    ns = seq // CHUNK
    hb = CHUNK // HALO_ROWS
    vec = lambda x: x.reshape(1, RW_WIDTH)
    cvec = pl.BlockSpec((1, RW_WIDTH), lambda b, s: (0, 0))
    return pl.pallas_call(
        _rwkv_kernel,
        out_shape=jax.ShapeDtypeStruct((t, RW_WIDTH), BF16),
        grid=(batch, ns),
        in_specs=[pl.BlockSpec((CHUNK, RW_PCOLS), lambda b, s: (b * ns + s, 0)),
                  pl.BlockSpec((HALO_ROWS, RW_PCOLS), lambda b, s: (jnp.maximum((b * ns + s) * hb - 1, 0), 0)),
                  pl.BlockSpec((1, RW_PCOLS), lambda b, s: (0, 0)),
                  cvec, cvec, cvec, cvec, cvec, cvec, cvec,
                  pl.BlockSpec((LORA_PAD, RW_WIDTH), lambda b, s: (0, 0)),
                  pl.BlockSpec((LORA_PAD, RW_WIDTH), lambda b, s: (0, 0)),
                  pl.BlockSpec((RW_GATE_LORA, RW_WIDTH), lambda b, s: (0, 0))],
        out_specs=pl.BlockSpec((CHUNK, RW_WIDTH), lambda b, s: (b * ns + s, 0)),
        scratch_shapes=[pltpu.VMEM((RW_HEADS // 2, LANES, LANES), F32)],
        compiler_params=_params(2),
        name="rwkv7",
    )(p_rw, p_rw, mu.reshape(1, RW_PCOLS), vec(w0), vec(a0), vec(k_k), vec(k_a), vec(r_k), vec(ln_w),
      vec(ln_b), w2, a2, g2)


def _router_kernel(x_ref, g_ref, wr_ref, br_ref, xn_ref, eid_ref, wgt_ref):
    x = x_ref[...]
    xn = x * lax.rsqrt(jnp.mean(x * x, axis=-1, keepdims=True) + NORM_EPS) * g_ref[...]
    xn_ref[...] = xn
    xh, xl = _split2(xn)
    wh, wl = _split2(wr_ref[...])
    logits = (jnp.dot(xh, wh, preferred_element_type=F32) + jnp.dot(xh, wl, preferred_element_type=F32)
              + jnp.dot(xl, wh, preferred_element_type=F32)) + br_ref[...]
    tm = logits.shape[0]
    lane = lax.broadcasted_iota(jnp.int32, (tm, LANES), 1)
    neg = -1e30
    big = 4 * LANES

    def first_argmax(vals):
        mx = jnp.max(vals, axis=-1, keepdims=True)
        idx = jnp.min(jnp.where(vals == mx, lane, big), axis=-1, keepdims=True)
        return mx, idx

    gl = jnp.where(lane < N_GROUPS, logits, neg)
    gmax, gi = first_argmax(gl)
    gp = 1.0 / jnp.sum(jnp.where(lane < N_GROUPS, jnp.exp(gl - gmax), 0.0), axis=-1, keepdims=True)
    e_lo = N_GROUPS + gi * EXPERTS_PER_GROUP
    el = jnp.where((lane >= e_lo) & (lane < e_lo + EXPERTS_PER_GROUP), logits, neg)
    m1, i1 = first_argmax(el)
    m2, i2 = first_argmax(jnp.where(lane == i1, neg, el))
    e2 = jnp.exp(m2 - m1)
    w1 = gp / (1.0 + e2)
    w2 = gp * e2 / (1.0 + e2)
    eid_ref[...] = jnp.where(lane == 0, i1 - N_GROUPS, jnp.where(lane == 1, i2 - N_GROUPS, 0))
    wgt_ref[...] = jnp.where(lane == 0, w1, jnp.where(lane == 1, w2, 0.0))


def _router(x_mid, norm_g, wr, br, tm=256):
    t, d = x_mid.shape
    tm = min(tm, t)
    return pl.pallas_call(
        _router_kernel,
        out_shape=(jax.ShapeDtypeStruct((t, d), F32),
                   jax.ShapeDtypeStruct((t, LANES), jnp.int32),
                   jax.ShapeDtypeStruct((t, LANES), F32)),
        grid=(t // tm,),
        in_specs=[pl.BlockSpec((tm, d), lambda i: (i, 0)), pl.BlockSpec((1, d), lambda i: (0, 0)),
                  pl.BlockSpec((d, LANES), lambda i: (0, 0)), pl.BlockSpec((1, LANES), lambda i: (0, 0))],
        out_specs=(pl.BlockSpec((tm, d), lambda i: (i, 0)), pl.BlockSpec((tm, LANES), lambda i: (i, 0)),
                   pl.BlockSpec((tm, LANES), lambda i: (i, 0))),
        compiler_params=_params(1),
        name="router",
    )(x_mid, norm_g.reshape(1, d), wr, br)


DMA_GROUP = 8


def _moe_up_kernel(be_ref, nv_ref, tok_ref, x_hbm, w1_ref, w3_ref, h_ref, xbuf, sem):
    del be_ref
    i = pl.program_id(0)
    nb = pl.num_programs(0)
    slot = i % 2

    def start_gather(blk, sl):
        def group(g, carry):
            for j in range(DMA_GROUP):
                r = g * DMA_GROUP + j
                tok = tok_ref[blk * MOE_ROWS + r]
                pltpu.make_async_copy(x_hbm.at[pl.ds(tok, 1)], xbuf.at[sl, pl.ds(r, 1)], sem.at[sl]).start()
            return carry
        lax.fori_loop(0, MOE_ROWS // DMA_GROUP, group, 0)

    @pl.when(jnp.logical_and(i == 0, nv_ref[0] > 0))
    def _():
        start_gather(0, 0)

    nxt = jnp.minimum(i + 1, nb - 1)

    @pl.when(jnp.logical_and(i + 1 < nb, nv_ref[nxt] > 0))
    def _():
        start_gather(nxt, 1 - slot)

    @pl.when(nv_ref[i] > 0)
    def _():
        pltpu.make_async_copy(x_hbm.at[pl.ds(0, MOE_ROWS)], xbuf.at[slot], sem.at[slot]).wait()
        x = xbuf[slot].astype(BF16)
        h1 = jnp.dot(x, w1_ref[0], preferred_element_type=F32)
        h3 = jnp.dot(x, w3_ref[0], preferred_element_type=F32)
        h_ref[...] = (_silu(h1) * h3).astype(h_ref.dtype)

    @pl.when(nv_ref[i] == 0)
    def _():
        h_ref[...] = jnp.zeros_like(h_ref)


def _moe_down_kernel(be_ref, nv_ref, dst_ref, h_ref, w2_ref, y_hbm, ybuf, sem):
    del be_ref
    i = pl.program_id(0)
    nb = pl.num_programs(0)
    slot = i % 2

    def row_copy(blk, sl, r):
        d = dst_ref[blk * MOE_ROWS + r]
        return pltpu.make_async_copy(ybuf.at[sl, pl.ds(r, 1)], y_hbm.at[pl.ds(d, 1)], sem.at[sl])

    def scatter(blk, sl, wait):
        n = nv_ref[blk]
        ng = lax.shift_right_logical(n, 3)

        def group(g, carry):
            if wait:
                r0 = pl.multiple_of(g * DMA_GROUP, DMA_GROUP)
                pltpu.make_async_copy(ybuf.at[sl, pl.ds(r0, DMA_GROUP)], y_hbm.at[pl.ds(0, DMA_GROUP)],
                                      sem.at[sl]).wait()
            else:
                for j in range(DMA_GROUP):
                    row_copy(blk, sl, g * DMA_GROUP + j).start()
            return carry

        def single(r, carry):
            cp = row_copy(blk, sl, r)
            if wait:
                cp.wait()
            else:
                cp.start()
            return carry

        lax.fori_loop(0, ng, group, 0)
        lax.fori_loop(ng * DMA_GROUP, n, single, 0)

    prev2 = jnp.maximum(i - 2, 0)

    @pl.when(i >= 2)
    def _():
        scatter(prev2, slot, True)

    @pl.when(nv_ref[i] > 0)
    def _():
        ybuf[slot] = jnp.dot(h_ref[...], w2_ref[0], preferred_element_type=F32)
        scatter(i, slot, False)

    prev1 = jnp.maximum(i - 1, 0)

    @pl.when(jnp.logical_and(i == nb - 1, i >= 1))
    def _():
        scatter(prev1, 1 - slot, True)

    @pl.when(i == nb - 1)
    def _():
        scatter(i, slot, True)


def _moe_experts(xn, eid, w1, w3, w2):
    t, d = xn.shape
    a = t * TOP_K
    e_flat = eid.reshape(a)
    tok_flat = jnp.repeat(jnp.arange(t, dtype=jnp.int32), TOP_K)
    dst_flat = (jnp.arange(a, dtype=jnp.int32) % TOP_K) * t + tok_flat
    onehot = (e_flat[:, None] == jnp.arange(N_EXPERTS, dtype=jnp.int32)[None, :]).astype(jnp.int32)
    csum = jnp.cumsum(onehot, axis=0)
    rank = jnp.sum(csum * onehot, axis=1) - 1
    counts = csum[-1]
    pcounts = (counts + MOE_ROWS - 1) // MOE_ROWS * MOE_ROWS
    pend = jnp.cumsum(pcounts)
    poffs = pend - pcounts
    dest = poffs[e_flat] + rank
    p_rows = a + N_EXPERTS * MOE_ROWS
    nb = p_rows // MOE_ROWS
    slot_tok = jnp.zeros((p_rows,), jnp.int32).at[dest].set(tok_flat)
    slot_dst = jnp.zeros((p_rows,), jnp.int32).at[dest].set(dst_flat)
    starts = jnp.arange(nb, dtype=jnp.int32) * MOE_ROWS
    block_e = jnp.minimum(jnp.searchsorted(pend, starts, side='right'), N_EXPERTS - 1).astype(jnp.int32)
    block_v = jnp.clip(poffs[block_e] + counts[block_e] - starts, 0, MOE_ROWS).astype(jnp.int32)
    last_e = block_e[jnp.maximum(jnp.sum(starts < pend[-1]) - 1, 0)]
    block_e = jnp.where(starts < pend[-1], block_e, last_e)

    h = pl.pallas_call(
        _moe_up_kernel,
        out_shape=jax.ShapeDtypeStruct((p_rows, D_EXPERT), BF16),
        grid_spec=pltpu.PrefetchScalarGridSpec(
            num_scalar_prefetch=3,
            grid=(nb,),
            in_specs=[pl.BlockSpec(memory_space=pl.ANY),
                      pl.BlockSpec((1, d, D_EXPERT), lambda i, be, bv, tk: (be[i], 0, 0)),
                      pl.BlockSpec((1, d, D_EXPERT), lambda i, be, bv, tk: (be[i], 0, 0))],
            out_specs=pl.BlockSpec((MOE_ROWS, D_EXPERT), lambda i, be, bv, tk: (i, 0)),
            scratch_shapes=[pltpu.VMEM((2, MOE_ROWS, d), F32), pltpu.SemaphoreType.DMA((2,))]),
        compiler_params=_params(1),
        name="moe_up",
    )(block_e, block_v, slot_tok, xn, w1, w3)

    y = pl.pallas_call(
        _moe_down_kernel,
        out_shape=jax.ShapeDtypeStruct((TOP_K * t, d), F32),
        grid_spec=pltpu.PrefetchScalarGridSpec(
            num_scalar_prefetch=3,
            grid=(nb,),
            in_specs=[pl.BlockSpec((MOE_ROWS, D_EXPERT), lambda i, be, bv, ds: (i, 0)),
                      pl.BlockSpec((1, D_EXPERT, d), lambda i, be, bv, ds: (be[i], 0, 0))],
            out_specs=pl.BlockSpec(memory_space=pl.ANY),
            scratch_shapes=[pltpu.VMEM((2, MOE_ROWS, d), F32), pltpu.SemaphoreType.DMA((2,))]),
        compiler_params=_params(1),
        name="moe_down",
    )(block_e, block_v, slot_dst, h, w2)
    return y


def _final_kernel(x_ref, y0_ref, y1_ref, wgt_ref, g_ref, o_ref):
    wgt = wgt_ref[...]
    x = x_ref[...] + wgt[:, 0:1] * y0_ref[...] + wgt[:, 1:2] * y1_ref[...]
    o_ref[...] = x * lax.rsqrt(jnp.mean(x * x, axis=-1, keepdims=True) + NORM_EPS) * g_ref[...]


def _final(x_mid, y, wgt, g, tm=256):
    t, d = x_mid.shape
    tm = min(tm, t)
    nblk = t // tm
    return pl.pallas_call(
        _final_kernel,
        out_shape=jax.ShapeDtypeStruct((t, d), F32),
        grid=(nblk,),
        in_specs=[pl.BlockSpec((tm, d), lambda i: (i, 0)), pl.BlockSpec((tm, d), lambda i: (i, 0)),
                  pl.BlockSpec((tm, d), lambda i: (i + nblk, 0)), pl.BlockSpec((tm, LANES), lambda i: (i, 0)),
                  pl.BlockSpec((1, d), lambda i: (0, 0))],
        out_specs=pl.BlockSpec((tm, d), lambda i: (i, 0)),
        compiler_params=_params(1),
        name="final_combine",
    )(x_mid, y, y, wgt, g.reshape(1, d))


def _pad_cols(w, n):
    return jnp.pad(w, ((0, 0), (0, n - w.shape[1])))


def _pad_rows(w, n):
    return jnp.pad(w, ((0, n - w.shape[0]), (0, 0)))


def _layer(x2, batch, seq, norm1_g, w_in, rw_mu, rw_w0, rw_w2, rw_a0, rw_a2, rw_g2, rw_k_k, rw_k_a, rw_r_k,
           rw_ln_w, rw_ln_b, dn_conv_w, dn_a_log, dn_dt_bias, dn_norm_w, gate_b, w_branch, w_out, norm2_g,
           moe_gr_w, moe_gr_b, moe_er_w, moe_er_b, moe_w1, moe_w3, moe_w2):
    sizes = (RW_WIDTH, RW_DECAY_LORA, RW_WIDTH, RW_WIDTH, RW_A_LORA, RW_GATE_LORA,
             DN_QKV, 2 * DN_HEADS, DN_VAL, 2 * D_MODEL)
    offs = [0]
    for sz in sizes:
        offs.append(offs[-1] + sz)
    cols = lambda w, i: w[..., offs[i]:offs[i + 1]]
    w_rw = jnp.concatenate([cols(w_in, 0), _pad_cols(cols(w_in, 1), LORA_PAD), cols(w_in, 2), cols(w_in, 3),
                            _pad_cols(cols(w_in, 4), LORA_PAD), cols(w_in, 5)], axis=1).astype(BF16)
    w_dn = jnp.concatenate([cols(w_in, 6), _pad_cols(cols(w_in, 7), LORA_PAD), cols(w_in, 8)],
                           axis=1).astype(BF16)
    w_gate = cols(w_in, 9).astype(BF16)
    mu2 = rw_mu.reshape(1, -1)
    mu = jnp.concatenate([cols(mu2, 0), _pad_cols(cols(mu2, 1), LORA_PAD), cols(mu2, 2), cols(mu2, 3),
                          _pad_cols(cols(mu2, 4), LORA_PAD), cols(mu2, 5)], axis=1)

    h = _rmsnorm(x2, norm1_g, BF16)
    p_rw = _matmul(h, w_rw, F32, 1024, 512, "proj_rwkv")
    p_dn = _matmul(h, w_dn, F32, 1024, 640, "proj_gdn")
    gates = _matmul(h, w_gate, BF16, 1024, 512, "proj_gate", bias=gate_b)

    o_rw = _rwkv7(p_rw, batch, seq, mu, rw_w0, _pad_rows(rw_w2, LORA_PAD).astype(BF16), rw_a0,
                  _pad_rows(rw_a2, LORA_PAD).astype(BF16), rw_g2.astype(BF16), rw_k_k, rw_k_a,
                  rw_r_k.reshape(-1), rw_ln_w, rw_ln_b)
    o_dn = _gated_deltanet(p_dn, batch, seq, dn_conv_w, dn_a_log, dn_dt_bias, dn_norm_w)

    mix = _branch_mix(o_rw, o_dn, w_branch[:RW_WIDTH].astype(BF16), w_branch[RW_WIDTH:].astype(BF16), gates)
    x_mid = _matmul(mix, w_out.astype(BF16), F32, 1024, 512, "proj_out", residual=x2)

    wr = _pad_cols(jnp.concatenate([moe_gr_w, moe_er_w], axis=1), LANES)
    br = _pad_cols(jnp.concatenate([moe_gr_b, moe_er_b]).reshape(1, -1), LANES)
    xn, eid, wgt = _router(x_mid, norm2_g, wr, br)
    y = _moe_experts(xn, eid[:, :TOP_K], moe_w1.astype(BF16), moe_w3.astype(BF16), moe_w2.astype(BF16))
    return x_mid, y, wgt


def kernel(x, norm1_g, w_in, rw_mu, rw_w0, rw_w2, rw_a0, rw_a2, rw_g2, rw_k_k, rw_k_a, rw_r_k, rw_ln_w, rw_ln_b, dn_conv_w, dn_a_log, dn_dt_bias, dn_norm_w, gate_b, w_branch, w_out, norm2_g, moe_gr_w, moe_gr_b, moe_er_w, moe_er_b, moe_w1, moe_w3, moe_w2, final_g):
    batch, seq, d = x.shape
    assert norm1_g.shape[0] == 1, "single-layer block"
    x_mid, y, wgt = _layer(
        x.reshape(batch * seq, d), batch, seq, norm1_g[0], w_in[0], rw_mu[0], rw_w0[0], rw_w2[0], rw_a0[0],
        rw_a2[0], rw_g2[0], rw_k_k[0], rw_k_a[0], rw_r_k[0], rw_ln_w[0], rw_ln_b[0], dn_conv_w[0], dn_a_log[0],
        dn_dt_bias[0], dn_norm_w[0], gate_b[0], w_branch[0], w_out[0], norm2_g[0], moe_gr_w[0], moe_gr_b[0],
        moe_er_w[0], moe_er_b[0], moe_w1[0], moe_w3[0], moe_w2[0])
    return _final(x_mid, y, wgt, final_g).reshape(batch, seq, d)
```

```python
import functools

import jax
import jax.numpy as jnp
from jax import lax
from jax.experimental import pallas as pl
from jax.experimental.pallas import tpu as pltpu

F32 = jnp.float32
BF16 = jnp.bfloat16

D_MODEL = 4096
RW_HEADS = 32
RW_HEAD = 64
RW_WIDTH = RW_HEADS * RW_HEAD
RW_DECAY_LORA = 96
RW_A_LORA = 96
RW_GATE_LORA = 256
RW_GN_EPS = 64e-5
RW_DECAY_SCALE = 0.606531
DN_HEADS = 16
DN_HEAD = 128
DN_KEY = DN_HEADS * DN_HEAD
DN_VAL = DN_HEADS * DN_HEAD
DN_QKV = 2 * DN_KEY + DN_VAL
DN_CONV = 4
N_GROUPS = 8
EXPERTS_PER_GROUP = 8
N_EXPERTS = N_GROUPS * EXPERTS_PER_GROUP
TOP_K = 2
D_EXPERT = 768
NORM_EPS = 1e-6

LANES = 128
HALO_ROWS = 8
CHUNK = 64
LORA_PAD = 128
RW_PCOLS = 3 * RW_WIDTH + 2 * LORA_PAD + RW_GATE_LORA
DN_PCOLS = DN_QKV + LORA_PAD + DN_VAL
RW_OFF_R = 0
RW_OFF_WD = RW_WIDTH
RW_OFF_K = RW_OFF_WD + LORA_PAD
RW_OFF_V = RW_OFF_K + RW_WIDTH
RW_OFF_AD = RW_OFF_V + RW_WIDTH
RW_OFF_GD = RW_OFF_AD + LORA_PAD
DN_OFF_SMALL = DN_QKV
DN_OFF_Z = DN_QKV + LORA_PAD
MOE_ROWS = 256
UP_TILES = 2
UP_TILE = D_EXPERT // UP_TILES
VMEM_LIMIT = 56 * 1024 * 1024


def _params(n_axes):
    return pltpu.CompilerParams(dimension_semantics=("arbitrary",) * n_axes,
                                vmem_limit_bytes=VMEM_LIMIT)


def _dot(a, b):
    return jnp.dot(a.astype(BF16), b.astype(BF16), preferred_element_type=F32)


def _dot_nt(a, b):
    return lax.dot_general(a.astype(BF16), b.astype(BF16), (((1,), (1,)), ((), ())),
                           preferred_element_type=F32)


def _split2(x):
    hi = x.astype(BF16)
    lo = (x - hi.astype(F32)).astype(BF16)
    return hi, lo


def _sigmoid(x):
    return 1.0 / (1.0 + jnp.exp(-x))


def _silu(x):
    return x * _sigmoid(x)


def _rmsnorm_kernel(x_ref, g_ref, o_ref):
    x = x_ref[...]
    ms = jnp.mean(x * x, axis=-1, keepdims=True)
    o_ref[...] = (x * lax.rsqrt(ms + NORM_EPS) * g_ref[...]).astype(o_ref.dtype)


def _rmsnorm(x, g, out_dtype, tm=256):
    t, d = x.shape
    tm = min(tm, t)
    return pl.pallas_call(
        _rmsnorm_kernel,
        out_shape=jax.ShapeDtypeStruct((t, d), out_dtype),
        grid=(t // tm,),
        in_specs=[pl.BlockSpec((tm, d), lambda i: (i, 0)), pl.BlockSpec((1, d), lambda i: (0, 0))],
        out_specs=pl.BlockSpec((tm, d), lambda i: (i, 0)),
        compiler_params=_params(1),
        name="rmsnorm",
    )(x, g.reshape(1, d))


def _mm_kernel(a_ref, b_ref, o_ref):
    o_ref[...] = jnp.dot(a_ref[...], b_ref[...], preferred_element_type=F32).astype(o_ref.dtype)


def _mm_gate_kernel(a_ref, b_ref, bias_ref, o_ref):
    acc = jnp.dot(a_ref[...], b_ref[...], preferred_element_type=F32)
    o_ref[...] = _sigmoid(acc + bias_ref[...]).astype(o_ref.dtype)


def _mm_res_kernel(a_ref, b_ref, x_ref, o_ref):
    o_ref[...] = x_ref[...] + jnp.dot(a_ref[...], b_ref[...], preferred_element_type=F32)


def _matmul(a, b, out_dtype, tm, tn, name, bias=None, residual=None):
    m, k = a.shape
    n = b.shape[1]
    tm = min(tm, m)
    in_specs = [pl.BlockSpec((tm, k), lambda i, j: (i, 0)), pl.BlockSpec((k, tn), lambda i, j: (0, j))]
    args = [a, b]
    kern = _mm_kernel
    if bias is not None:
        in_specs.append(pl.BlockSpec((1, tn), lambda i, j: (0, j)))
        args.append(bias.reshape(1, n))
        kern = _mm_gate_kernel
    if residual is not None:
        in_specs.append(pl.BlockSpec((tm, tn), lambda i, j: (i, j)))
        args.append(residual)
        kern = _mm_res_kernel
    return pl.pallas_call(
        kern,
        out_shape=jax.ShapeDtypeStruct((m, n), out_dtype),
        grid=(m // tm, n // tn),
        in_specs=in_specs,
        out_specs=pl.BlockSpec((tm, tn), lambda i, j: (i, j)),
        compiler_params=_params(2),
        name=name,
    )(*args)


def _mix_kernel(a1_ref, b1_ref, a2_ref, b2_ref, g1_ref, g2_ref, o_ref):
    y1 = jnp.dot(a1_ref[...], b1_ref[...], preferred_element_type=F32)
    y2 = jnp.dot(a2_ref[...], b2_ref[...], preferred_element_type=F32)
    o_ref[...] = (g1_ref[...].astype(F32) * y1 + g2_ref[...].astype(F32) * y2).astype(o_ref.dtype)


def _branch_mix(o_rw, o_dn, wb_rw, wb_dn, gates, tm=1024, tn=512):
    m, k = o_rw.shape
    n = wb_rw.shape[1]
    tm = min(tm, m)
    goff = n // tn
    return pl.pallas_call(
        _mix_kernel,
        out_shape=jax.ShapeDtypeStruct((m, n), BF16),
        grid=(m // tm, n // tn),
        in_specs=[pl.BlockSpec((tm, k), lambda i, j: (i, 0)), pl.BlockSpec((k, tn), lambda i, j: (0, j)),
                  pl.BlockSpec((tm, k), lambda i, j: (i, 0)), pl.BlockSpec((k, tn), lambda i, j: (0, j)),
                  pl.BlockSpec((tm, tn), lambda i, j: (i, j)),
                  pl.BlockSpec((tm, tn), lambda i, j: (i, j + goff))],
        out_specs=pl.BlockSpec((tm, tn), lambda i, j: (i, j)),
        compiler_params=_params(2),
        name="branch_mix",
    )(o_rw, wb_rw, o_dn, wb_dn, gates, gates)


def _tri_incl():
    r = lax.broadcasted_iota(jnp.int32, (CHUNK, CHUNK), 0)
    c = lax.broadcasted_iota(jnp.int32, (CHUNK, CHUNK), 1)
    return (r >= c).astype(BF16)


def _eye_bf16(n):
    r = lax.broadcasted_iota(jnp.int32, (n, n), 0)
    c = lax.broadcasted_iota(jnp.int32, (n, n), 1)
    return (r == c).astype(BF16)


def _cumsum_rows(x):
    tri = _tri_incl()
    hi, lo = _split2(x)
    return (jnp.dot(tri, hi, preferred_element_type=F32) + jnp.dot(tri, lo, preferred_element_type=F32))


def _gdn_kernel(p_ref, halo_ref, cw_ref, alog_ref, dtb_ref, nw_ref, o_ref, ext_ref, state_ref):
    s = pl.program_id(1)

    @pl.when(s == 0)
    def _():
        state_ref[...] = jnp.zeros_like(state_ref)

    ext_ref[0:HALO_ROWS, :] = jnp.where(s == 0, 0.0, halo_ref[:, 0:DN_QKV])
    ext_ref[HALO_ROWS:HALO_ROWS + CHUNK, :] = p_ref[:, 0:DN_QKV]

    def conv_silu(col):
        cs = slice(col, col + DN_HEAD)
        acc = ext_ref[HALO_ROWS:HALO_ROWS + CHUNK, cs] * cw_ref[DN_CONV - 1:DN_CONV, cs]
        for j in range(DN_CONV - 1):
            r0 = HALO_ROWS - (DN_CONV - 1) + j
            acc = acc + ext_ref[r0:r0 + CHUNK, cs] * cw_ref[j:j + 1, cs]
        return _silu(acc)

    def l2norm(t):
        return t * lax.rsqrt(jnp.sum(t * t, axis=-1, keepdims=True) + NORM_EPS)

    small = p_ref[:, DN_OFF_SMALL:DN_OFF_SMALL + LORA_PAD]
    beta_all = _sigmoid(small)
    sp_in = small + dtb_ref[...]
    softplus = jnp.maximum(sp_in, 0.0) + jnp.log1p(jnp.exp(-jnp.abs(sp_in)))
    g_all = -jnp.exp(alog_ref[...]) * softplus
    gcum = _cumsum_rows(g_all)
    eye = _eye_bf16(LANES)
    ghi, glo = _split2(gcum)
    gcum_t = (lax.dot_general(eye, ghi, (((1,), (1,)), ((), ())), preferred_element_type=F32)
              + lax.dot_general(eye, glo, (((1,), (1,)), ((), ())), preferred_element_type=F32))

    r = lax.broadcasted_iota(jnp.int32, (CHUNK, CHUNK), 0)
    c = lax.broadcasted_iota(jnp.int32, (CHUNK, CHUNK), 1)
    causal = r >= c
    strict = r > c
    eye_c = jnp.where(r == c, 1.0, 0.0)
    nw = nw_ref[...]
    heads = range(DN_HEADS)

    q = [l2norm(conv_silu(h * DN_HEAD)) * (DN_HEAD ** -0.5) for h in heads]
    k = [l2norm(conv_silu(DN_KEY + h * DN_HEAD)) for h in heads]
    v = [conv_silu(2 * DN_KEY + h * DN_HEAD) for h in heads]
    gcol = [gcum[:, DN_HEADS + h:DN_HEADS + h + 1] for h in heads]
    glast = [g[CHUNK - 1:CHUNK, :] for g in gcol]
    decay = [jnp.where(causal, jnp.exp(jnp.where(causal, gcol[h] - gcum_t[DN_HEADS + h:DN_HEADS + h + 1, :], 0.0)),
                       0.0) for h in heads]
    beta = [beta_all[:, h:h + 1] for h in heads]
    kb = [k[h] * beta[h] for h in heads]
    eg = [jnp.exp(g) for g in gcol]
    kk = [_dot_nt(jnp.concatenate([kb[h], q[h]], axis=0), k[h]) for h in heads]
    k_tail_t = [_dot_nt(eye, k[h] * jnp.exp(glast[h] - gcol[h])) for h in heads]
    attn = [kk[h][CHUNK:] * decay[h] for h in heads]
    m = [-jnp.where(strict, kk[h][:CHUNK] * decay[h], 0.0) for h in heads]
    inv = [eye_c + m[h] for h in heads]
    m = [_dot(m[h], m[h]) for h in heads]
    for _ in range(4):
        both = [_dot(jnp.concatenate([m[h], inv[h]], axis=0), m[h]) for h in heads]
        inv = [inv[h] + both[h][CHUNK:] for h in heads]
        m = [both[h][:CHUNK] for h in heads]
    inv = [inv[h] + _dot(inv[h], m[h]) for h in heads]
    sol = [_dot(inv[h], jnp.concatenate([v[h] * beta[h], kb[h] * eg[h]], axis=1)) for h in heads]
    st = [state_ref[h] for h in heads]
    wq = [_dot(jnp.concatenate([sol[h][:, DN_HEAD:], q[h] * eg[h]], axis=0), st[h]) for h in heads]
    v_new = [sol[h][:, :DN_HEAD] - wq[h][:CHUNK] for h in heads]
    o = [wq[h][CHUNK:] + _dot(attn[h], v_new[h]) for h in heads]
    for h in heads:
        state_ref[h] = st[h] * jnp.exp(glast[h]) + _dot(k_tail_t[h], v_new[h])
    for h in heads:
        on = o[h] * lax.rsqrt(jnp.mean(o[h] * o[h], axis=-1, keepdims=True) + NORM_EPS) * nw
        z = p_ref[:, DN_OFF_Z + h * DN_HEAD:DN_OFF_Z + (h + 1) * DN_HEAD]
        o_ref[:, h * DN_HEAD:(h + 1) * DN_HEAD] = (on * _silu(z)).astype(o_ref.dtype)


def _gated_deltanet(p_dn, batch, seq, conv_w, a_log, dt_bias, norm_w):
    t = batch * seq
    ns = seq // CHUNK
    hb = CHUNK // HALO_ROWS
    alog = jnp.zeros((1, LORA_PAD), F32).at[0, DN_HEADS:2 * DN_HEADS].set(a_log)
    dtb = jnp.zeros((1, LORA_PAD), F32).at[0, DN_HEADS:2 * DN_HEADS].set(dt_bias)
    return pl.pallas_call(
        _gdn_kernel,
        out_shape=jax.ShapeDtypeStruct((t, DN_VAL), BF16),
        grid=(batch, ns),
        in_specs=[pl.BlockSpec((CHUNK, DN_PCOLS), lambda b, s: (b * ns + s, 0)),
                  pl.BlockSpec((HALO_ROWS, DN_PCOLS), lambda b, s: (jnp.maximum((b * ns + s) * hb - 1, 0), 0)),
                  pl.BlockSpec((DN_CONV, DN_QKV), lambda b, s: (0, 0)),
                  pl.BlockSpec((1, LORA_PAD), lambda b, s: (0, 0)),
                  pl.BlockSpec((1, LORA_PAD), lambda b, s: (0, 0)),
                  pl.BlockSpec((1, DN_HEAD), lambda b, s: (0, 0))],
        out_specs=pl.BlockSpec((CHUNK, DN_VAL), lambda b, s: (b * ns + s, 0)),
        scratch_shapes=[pltpu.VMEM((HALO_ROWS + CHUNK, DN_QKV), F32),
                        pltpu.VMEM((DN_HEADS, DN_HEAD, DN_HEAD), F32)],
        compiler_params=_params(2),
        name="gated_deltanet",
    )(p_dn, p_dn, conv_w, alog, dtb, norm_w.reshape(1, DN_HEAD))


def _rwkv_kernel(p_ref, halo_ref, mu_ref, w0_ref, a0_ref, kk_ref, ka_ref, rk_ref, lnw_ref, lnb_ref,
                 w2_ref, a2_ref, g2_ref, o_ref, state_ref):
    s = pl.program_id(1)

    @pl.when(s == 0)
    def _():
        state_ref[...] = jnp.zeros_like(state_ref)

    row1 = lax.broadcasted_iota(jnp.int32, (CHUNK, 1), 0)

    def shifted(lo, hi):
        cur = p_ref[:, lo:hi]
        last = jnp.where(s == 0, 0.0, halo_ref[HALO_ROWS - 1:HALO_ROWS, lo:hi])
        prev = jnp.where(row1 == 0, last, pltpu.roll(cur, 1, 0))
        return cur + (prev - cur) * mu_ref[:, lo:hi]

    wd = shifted(RW_OFF_WD, RW_OFF_WD + LORA_PAD)
    ad = shifted(RW_OFF_AD, RW_OFF_AD + LORA_PAD)
    gd = shifted(RW_OFF_GD, RW_OFF_GD + RW_GATE_LORA)
    log_w = -RW_DECAY_SCALE * _sigmoid(w0_ref[...] + _dot(jnp.tanh(wd), w2_ref[...]))
    a_all = _sigmoid(a0_ref[...] + _dot(ad, a2_ref[...]))
    g_all = _dot(_sigmoid(gd), g2_ref[...])

    lane = lax.broadcasted_iota(jnp.int32, (CHUNK, LANES), 1)
    row = lax.broadcasted_iota(jnp.int32, (CHUNK, LANES), 0)
    tcol = lane % RW_HEAD
    strict_ss = row > tcol
    incl_ss = row >= tcol
    eye_ss = jnp.where(row == tcol, 1.0, 0.0)
    r2 = lax.broadcasted_iota(jnp.int32, (LANES, LANES), 0)
    c2 = lax.broadcasted_iota(jnp.int32, (LANES, LANES), 1)
    bd_mask = (r2 // RW_HEAD) == (c2 // RW_HEAD)
    ones_bd = bd_mask.astype(BF16)
    head0 = lane < RW_HEAD
    eye = _eye_bf16(LANES)

    def seg_sum(x):
        s0 = jnp.sum(jnp.where(head0, x, 0.0), axis=-1, keepdims=True)
        s1 = jnp.sum(jnp.where(head0, 0.0, x), axis=-1, keepdims=True)
        return jnp.where(head0, s0, s1)

    def block_diag(x):
        return jnp.where(bd_mask, jnp.concatenate([x, x], axis=0), 0.0)

    def per_head_rows(x):
        return jnp.concatenate([jnp.where(head0, x, 0.0), jnp.where(head0, 0.0, x)], axis=0)

    pairs = range(RW_HEADS // 2)
    sl = [slice(p * LANES, (p + 1) * LANES) for p in pairs]
    r = [shifted(RW_OFF_R + p * LANES, RW_OFF_R + (p + 1) * LANES) for p in pairs]
    k = [shifted(RW_OFF_K + p * LANES, RW_OFF_K + (p + 1) * LANES) for p in pairs]
    v = [shifted(RW_OFF_V + p * LANES, RW_OFF_V + (p + 1) * LANES) for p in pairs]
    lw = [log_w[:, sl[p]] for p in pairs]
    a = [a_all[:, sl[p]] for p in pairs]
    kkr = [k[p] * kk_ref[:, sl[p]] for p in pairs]
    kss = [seg_sum(kkr[p] * kkr[p]) for p in pairs]
    gc = [_cumsum_rows(lw[p]) for p in pairs]
    kk = [kkr[p] * lax.rsqrt(kss[p] + NORM_EPS) for p in pairs]
    kh = [k[p] * (1.0 + (a[p] - 1.0) * ka_ref[:, sl[p]]) for p in pairs]
    b = [kk[p] * a[p] for p in pairs]
    bonus_s = [seg_sum(r[p] * kh[p] * rk_ref[:, sl[p]]) for p in pairs]
    glast = [g[CHUNK - 1:CHUNK, :] for g in gc]
    e_neg = [jnp.exp(-g) for g in gc]
    e_tail = [jnp.exp(glast[p] - gc[p]) for p in pairs]
    x1 = [jnp.concatenate([kk[p] * jnp.exp(gc[p] - lw[p]), r[p] * jnp.exp(gc[p])], axis=0) for p in pairs]
    x3 = [jnp.concatenate([kh[p] * e_tail[p], b[p] * e_tail[p]], axis=0) for p in pairs]
    st = [state_ref[p] for p in pairs]
    akb = [_dot_nt(x1[p], jnp.concatenate([per_head_rows(kh[p] * e_neg[p]), per_head_rows(b[p] * e_neg[p]),
                                           st[p]], axis=0)) for p in pairs]
    ak = [t[:, 0:LANES] for t in akb]
    ab = [t[:, LANES:2 * LANES] for t in akb]
    xh = [t[:, 2 * LANES:3 * LANES] for t in akb]
    bd_v = [block_diag(v[p]) for p in pairs]
    rhs = [xh[p][:CHUNK] + _dot(jnp.where(strict_ss, ak[p][:CHUNK], 0.0), bd_v[p]) for p in pairs]
    m = [-jnp.where(strict_ss, ab[p][:CHUNK], 0.0) for p in pairs]
    inv = [eye_ss + m[p] for p in pairs]
    m = [_dot(m[p], block_diag(m[p])) for p in pairs]
    for _ in range(4):
        both = [_dot(jnp.concatenate([m[p], inv[p]], axis=0), block_diag(m[p])) for p in pairs]
        inv = [inv[p] + both[p][CHUNK:] for p in pairs]
        m = [both[p][:CHUNK] for p in pairs]
    inv = [inv[p] + _dot(inv[p], block_diag(m[p])) for p in pairs]
    u = [_dot(inv[p], block_diag(rhs[p])) for p in pairs]
    o = [xh[p][CHUNK:] + _dot(jnp.concatenate([jnp.where(incl_ss, ak[p][CHUNK:], 0.0),
                                               jnp.where(incl_ss, ab[p][CHUNK:], 0.0)], axis=1),
                              jnp.concatenate([bd_v[p], -block_diag(u[p])], axis=0)) for p in pairs]
    vu_t = [jnp.concatenate([v[p], -u[p]], axis=0).T for p in pairs]
    for p in pairs:
        state_ref[p] = st[p] * jnp.exp(glast[p]) + jnp.where(bd_mask, _dot(vu_t[p], x3[p]), 0.0)
    oc = [o[p] - seg_sum(o[p]) * (1.0 / RW_HEAD) for p in pairs]
    var = [seg_sum(oc[p] * oc[p]) * (1.0 / RW_HEAD) for p in pairs]
    for p in pairs:
        on = oc[p] * lax.rsqrt(var[p] + RW_GN_EPS) * lnw_ref[:, sl[p]] + lnb_ref[:, sl[p]]
        o_ref[:, sl[p]] = ((on + bonus_s[p] * v[p]) * g_all[:, sl[p]]).astype(o_ref.dtype)


def _rwkv7(p_rw, batch, seq, mu, w0, w2, a0, a2, g2, k_k, k_a, r_k, ln_w, ln_b):
    t = batch * seq
    ns = seq // CHUNK
    hb = CHUNK // HALO_ROWS
    vec = lambda x: x.reshape(1, RW_WIDTH)
    cvec = pl.BlockSpec((1, RW_WIDTH), lambda b, s: (0, 0))
    return pl.pallas_call(
        _rwkv_kernel,
        out_shape=jax.ShapeDtypeStruct((t, RW_WIDTH), BF16),
        grid=(batch, ns),
        in_specs=[pl.BlockSpec((CHUNK, RW_PCOLS), lambda b, s: (b * ns + s, 0)),
                  pl.BlockSpec((HALO_ROWS, RW_PCOLS), lambda b, s: (jnp.maximum((b * ns + s) * hb - 1, 0), 0)),
                  pl.BlockSpec((1, RW_PCOLS), lambda b, s: (0, 0)),
                  cvec, cvec, cvec, cvec, cvec, cvec, cvec,
                  pl.BlockSpec((LORA_PAD, RW_WIDTH), lambda b, s: (0, 0)),
                  pl.BlockSpec((LORA_PAD, RW_WIDTH), lambda b, s: (0, 0)),
                  pl.BlockSpec((RW_GATE_LORA, RW_WIDTH), lambda b, s: (0, 0))],
        out_specs=pl.BlockSpec((CHUNK, RW_WIDTH), lambda b, s: (b * ns + s, 0)),
        scratch_shapes=[pltpu.VMEM((RW_HEADS // 2, LANES, LANES), F32)],
        compiler_params=_params(2),
        name="rwkv7",
    )(p_rw, p_rw, mu.reshape(1, RW_PCOLS), vec(w0), vec(a0), vec(k_k), vec(k_a), vec(r_k), vec(ln_w),
      vec(ln_b), w2, a2, g2)


def _router_kernel(x_ref, g_ref, wr_ref, br_ref, xn_ref, eid_ref, wgt_ref):
    x = x_ref[...]
    xn = x * lax.rsqrt(jnp.mean(x * x, axis=-1, keepdims=True) + NORM_EPS) * g_ref[...]
    xn_ref[...] = xn
    xh, xl = _split2(xn)
    wh, wl = _split2(wr_ref[...])
    logits = (jnp.dot(xh, wh, preferred_element_type=F32) + jnp.dot(xh, wl, preferred_element_type=F32)
              + jnp.dot(xl, wh, preferred_element_type=F32)) + br_ref[...]
    tm = logits.shape[0]
    lane = lax.broadcasted_iota(jnp.int32, (tm, LANES), 1)
    neg = -1e30
    big = 4 * LANES

    def first_argmax(vals):
        mx = jnp.max(vals, axis=-1, keepdims=True)
        idx = jnp.min(jnp.where(vals == mx, lane, big), axis=-1, keepdims=True)
        return mx, idx

    gl = jnp.where(lane < N_GROUPS, logits, neg)
    gmax, gi = first_argmax(gl)
    gp = 1.0 / jnp.sum(jnp.where(lane < N_GROUPS, jnp.exp(gl - gmax), 0.0), axis=-1, keepdims=True)
    e_lo = N_GROUPS + gi * EXPERTS_PER_GROUP
    el = jnp.where((lane >= e_lo) & (lane < e_lo + EXPERTS_PER_GROUP), logits, neg)
    m1, i1 = first_argmax(el)
    m2, i2 = first_argmax(jnp.where(lane == i1, neg, el))
    e2 = jnp.exp(m2 - m1)
    w1 = gp / (1.0 + e2)
    w2 = gp * e2 / (1.0 + e2)
    eid_ref[...] = jnp.where(lane == 0, i1 - N_GROUPS, jnp.where(lane == 1, i2 - N_GROUPS, 0))
    wgt_ref[...] = jnp.where(lane == 0, w1, jnp.where(lane == 1, w2, 0.0))


def _router(x_mid, norm_g, wr, br, tm=256):
    t, d = x_mid.shape
    tm = min(tm, t)
    return pl.pallas_call(
        _router_kernel,
        out_shape=(jax.ShapeDtypeStruct((t, d), F32),
                   jax.ShapeDtypeStruct((t, LANES), jnp.int32),
                   jax.ShapeDtypeStruct((t, LANES), F32)),
        grid=(t // tm,),
        in_specs=[pl.BlockSpec((tm, d), lambda i: (i, 0)), pl.BlockSpec((1, d), lambda i: (0, 0)),
                  pl.BlockSpec((d, LANES), lambda i: (0, 0)), pl.BlockSpec((1, LANES), lambda i: (0, 0))],
        out_specs=(pl.BlockSpec((tm, d), lambda i: (i, 0)), pl.BlockSpec((tm, LANES), lambda i: (i, 0)),
                   pl.BlockSpec((tm, LANES), lambda i: (i, 0))),
        compiler_params=_params(1),
        name="router",
    )(x_mid, norm_g.reshape(1, d), wr, br)


DMA_GROUP = 8


def _expert_changed(be_ref, i):
    return jnp.logical_or(i == 0, be_ref[i] != be_ref[jnp.maximum(i - 1, 0)])


def _moe_up_kernel(be_ref, nv_ref, tok_ref, x_hbm, w1_ref, w3_ref, h_ref, xbuf, wb1, wb3, sem):
    i = pl.program_id(1)
    nb = pl.num_programs(1)
    slot = i % 2

    def start_gather(blk, sl):
        def group(g, carry):
            for j in range(DMA_GROUP):
                r = g * DMA_GROUP + j
                tok = tok_ref[blk * MOE_ROWS + r]
                pltpu.make_async_copy(x_hbm.at[pl.ds(tok, 1)], xbuf.at[sl, pl.ds(r, 1)], sem.at[sl]).start()
            return carry
        lax.fori_loop(0, MOE_ROWS // DMA_GROUP, group, 0)

    @pl.when(jnp.logical_and(i == 0, nv_ref[0] > 0))
    def _():
        start_gather(0, 0)

    nxt = jnp.minimum(i + 1, nb - 1)

    @pl.when(jnp.logical_and(i + 1 < nb, nv_ref[nxt] > 0))
    def _():
        start_gather(nxt, 1 - slot)

    @pl.when(nv_ref[i] > 0)
    def _():
        @pl.when(_expert_changed(be_ref, i))
        def _():
            wb1[...] = w1_ref[0].astype(BF16)
            wb3[...] = w3_ref[0].astype(BF16)

        pltpu.make_async_copy(x_hbm.at[pl.ds(0, MOE_ROWS)], xbuf.at[slot], sem.at[slot]).wait()
        x = xbuf[slot].astype(BF16)
        h1 = jnp.dot(x, wb1[...], preferred_element_type=F32)
        h3 = jnp.dot(x, wb3[...], preferred_element_type=F32)
        h_ref[...] = (_silu(h1) * h3).astype(h_ref.dtype)

    @pl.when(nv_ref[i] == 0)
    def _():
        h_ref[...] = jnp.zeros_like(h_ref)


def _moe_down_kernel(be_ref, nv_ref, dst_ref, h_ref, w2_ref, y_hbm, ybuf, wb2, sem):
    i = pl.program_id(0)
    nb = pl.num_programs(0)
    slot = i % 2

    def row_copy(blk, sl, r):
        d = dst_ref[blk * MOE_ROWS + r]
        return pltpu.make_async_copy(ybuf.at[sl, pl.ds(r, 1)], y_hbm.at[pl.ds(d, 1)], sem.at[sl])

    def scatter(blk, sl, wait):
        n = nv_ref[blk]
        ng = lax.shift_right_logical(n, 3)

        def group(g, carry):
            if wait:
                r0 = pl.multiple_of(g * DMA_GROUP, DMA_GROUP)
                pltpu.make_async_copy(ybuf.at[sl, pl.ds(r0, DMA_GROUP)], y_hbm.at[pl.ds(0, DMA_GROUP)],
                                      sem.at[sl]).wait()
            else:
                for j in range(DMA_GROUP):
                    row_copy(blk, sl, g * DMA_GROUP + j).start()
            return carry

        def single(r, carry):
            cp = row_copy(blk, sl, r)
            if wait:
                cp.wait()
            else:
                cp.start()
            return carry

        lax.fori_loop(0, ng, group, 0)
        lax.fori_loop(ng * DMA_GROUP, n, single, 0)

    prev2 = jnp.maximum(i - 2, 0)

    @pl.when(i >= 2)
    def _():
        scatter(prev2, slot, True)

    @pl.when(nv_ref[i] > 0)
    def _():
        @pl.when(_expert_changed(be_ref, i))
        def _():
            wb2[...] = w2_ref[0].astype(BF16)

        ybuf[slot] = jnp.dot(h_ref[...], wb2[...], preferred_element_type=F32)
        scatter(i, slot, False)

    prev1 = jnp.maximum(i - 1, 0)

    @pl.when(jnp.logical_and(i == nb - 1, i >= 1))
    def _():
        scatter(prev1, 1 - slot, True)

    @pl.when(i == nb - 1)
    def _():
        scatter(i, slot, True)


def _moe_experts(xn, eid, w1, w3, w2):
    t, d = xn.shape
    a = t * TOP_K
    e_flat = eid.reshape(a)
    tok_flat = jnp.repeat(jnp.arange(t, dtype=jnp.int32), TOP_K)
    dst_flat = (jnp.arange(a, dtype=jnp.int32) % TOP_K) * t + tok_flat
    onehot = (e_flat[:, None] == jnp.arange(N_EXPERTS, dtype=jnp.int32)[None, :]).astype(jnp.int32)
    csum = jnp.cumsum(onehot, axis=0)
    rank = jnp.sum(csum * onehot, axis=1) - 1
    counts = csum[-1]
    pcounts = (counts + MOE_ROWS - 1) // MOE_ROWS * MOE_ROWS
    pend = jnp.cumsum(pcounts)
    poffs = pend - pcounts
    dest = poffs[e_flat] + rank
    p_rows = a + N_EXPERTS * MOE_ROWS
    nb = p_rows // MOE_ROWS
    slot_tok = jnp.zeros((p_rows,), jnp.int32).at[dest].set(tok_flat)
    slot_dst = jnp.zeros((p_rows,), jnp.int32).at[dest].set(dst_flat)
    starts = jnp.arange(nb, dtype=jnp.int32) * MOE_ROWS
    block_e = jnp.minimum(jnp.searchsorted(pend, starts, side='right'), N_EXPERTS - 1).astype(jnp.int32)
    block_v = jnp.clip(poffs[block_e] + counts[block_e] - starts, 0, MOE_ROWS).astype(jnp.int32)
    last_e = block_e[jnp.maximum(jnp.sum(starts < pend[-1]) - 1, 0)]
    block_e = jnp.where(starts < pend[-1], block_e, last_e)

    h = pl.pallas_call(
        _moe_up_kernel,
        out_shape=jax.ShapeDtypeStruct((p_rows, D_EXPERT), BF16),
        grid_spec=pltpu.PrefetchScalarGridSpec(
            num_scalar_prefetch=3,
            grid=(UP_TILES, nb),
            in_specs=[pl.BlockSpec(memory_space=pl.ANY),
                      pl.BlockSpec((1, d, UP_TILE), lambda j, i, be, bv, tk: (be[i], 0, j)),
                      pl.BlockSpec((1, d, UP_TILE), lambda j, i, be, bv, tk: (be[i], 0, j))],
            out_specs=pl.BlockSpec((MOE_ROWS, UP_TILE), lambda j, i, be, bv, tk: (i, j)),
            scratch_shapes=[pltpu.VMEM((2, MOE_ROWS, d), F32), pltpu.VMEM((d, UP_TILE), BF16),
                            pltpu.VMEM((d, UP_TILE), BF16), pltpu.SemaphoreType.DMA((2,))]),
        compiler_params=_params(2),
        name="moe_up",
    )(block_e, block_v, slot_tok, xn, w1, w3)

    y = pl.pallas_call(
        _moe_down_kernel,
        out_shape=jax.ShapeDtypeStruct((TOP_K * t, d), F32),
        grid_spec=pltpu.PrefetchScalarGridSpec(
            num_scalar_prefetch=3,
            grid=(nb,),
            in_specs=[pl.BlockSpec((MOE_ROWS, D_EXPERT), lambda i, be, bv, ds: (i, 0)),
                      pl.BlockSpec((1, D_EXPERT, d), lambda i, be, bv, ds: (be[i], 0, 0))],
            out_specs=pl.BlockSpec(memory_space=pl.ANY),
            scratch_shapes=[pltpu.VMEM((2, MOE_ROWS, d), F32), pltpu.VMEM((D_EXPERT, d), BF16),
                            pltpu.SemaphoreType.DMA((2,))]),
        compiler_params=_params(1),
        name="moe_down",
    )(block_e, block_v, slot_dst, h, w2)
    return y


def _final_kernel(x_ref, y0_ref, y1_ref, wgt_ref, g_ref, o_ref):
    wgt = wgt_ref[...]
    x = x_ref[...] + wgt[:, 0:1] * y0_ref[...] + wgt[:, 1:2] * y1_ref[...]
    o_ref[...] = x * lax.rsqrt(jnp.mean(x * x, axis=-1, keepdims=True) + NORM_EPS) * g_ref[...]


def _final(x_mid, y, wgt, g, tm=256):
    t, d = x_mid.shape
    tm = min(tm, t)
    nblk = t // tm
    return pl.pallas_call(
        _final_kernel,
        out_shape=jax.ShapeDtypeStruct((t, d), F32),
        grid=(nblk,),
        in_specs=[pl.BlockSpec((tm, d), lambda i: (i, 0)), pl.BlockSpec((tm, d), lambda i: (i, 0)),
                  pl.BlockSpec((tm, d), lambda i: (i + nblk, 0)), pl.BlockSpec((tm, LANES), lambda i: (i, 0)),
                  pl.BlockSpec((1, d), lambda i: (0, 0))],
        out_specs=pl.BlockSpec((tm, d), lambda i: (i, 0)),
        compiler_params=_params(1),
        name="final_combine",
    )(x_mid, y, y, wgt, g.reshape(1, d))


def _pad_cols(w, n):
    return jnp.pad(w, ((0, 0), (0, n - w.shape[1])))


def _pad_rows(w, n):
    return jnp.pad(w, ((0, n - w.shape[0]), (0, 0)))


def _layer(x2, batch, seq, norm1_g, w_in, rw_mu, rw_w0, rw_w2, rw_a0, rw_a2, rw_g2, rw_k_k, rw_k_a, rw_r_k,
           rw_ln_w, rw_ln_b, dn_conv_w, dn_a_log, dn_dt_bias, dn_norm_w, gate_b, w_branch, w_out, norm2_g,
           moe_gr_w, moe_gr_b, moe_er_w, moe_er_b, moe_w1, moe_w3, moe_w2):
    sizes = (RW_WIDTH, RW_DECAY_LORA, RW_WIDTH, RW_WIDTH, RW_A_LORA, RW_GATE_LORA,
             DN_QKV, 2 * DN_HEADS, DN_VAL, 2 * D_MODEL)
    offs = [0]
    for sz in sizes:
        offs.append(offs[-1] + sz)
    cols = lambda w, i: w[..., offs[i]:offs[i + 1]]
    w_bf = w_in.astype(BF16)
    w_rw = jnp.concatenate([cols(w_bf, 0), _pad_cols(cols(w_bf, 1), LORA_PAD), cols(w_bf, 2), cols(w_bf, 3),
                            _pad_cols(cols(w_bf, 4), LORA_PAD), cols(w_bf, 5)], axis=1)
    w_dn = jnp.concatenate([cols(w_bf, 6), _pad_cols(cols(w_bf, 7), LORA_PAD), cols(w_bf, 8)], axis=1)
    w_gate = cols(w_bf, 9)
    mu2 = rw_mu.reshape(1, -1)
    mu = jnp.concatenate([cols(mu2, 0), _pad_cols(cols(mu2, 1), LORA_PAD), cols(mu2, 2), cols(mu2, 3),
                          _pad_cols(cols(mu2, 4), LORA_PAD), cols(mu2, 5)], axis=1)

    h = _rmsnorm(x2, norm1_g, BF16)
    p_rw = _matmul(h, w_rw, F32, 1024, 512, "proj_rwkv")
    p_dn = _matmul(h, w_dn, F32, 1024, 640, "proj_gdn")
    gates = _matmul(h, w_gate, BF16, 1024, 512, "proj_gate", bias=gate_b)

    o_rw = _rwkv7(p_rw, batch, seq, mu, rw_w0, _pad_rows(rw_w2, LORA_PAD).astype(BF16), rw_a0,
                  _pad_rows(rw_a2, LORA_PAD).astype(BF16), rw_g2.astype(BF16), rw_k_k, rw_k_a,
                  rw_r_k.reshape(-1), rw_ln_w, rw_ln_b)
    o_dn = _gated_deltanet(p_dn, batch, seq, dn_conv_w, dn_a_log, dn_dt_bias, dn_norm_w)

    mix = _branch_mix(o_rw, o_dn, w_branch[:RW_WIDTH].astype(BF16), w_branch[RW_WIDTH:].astype(BF16), gates)
    x_mid = _matmul(mix, w_out.astype(BF16), F32, 1024, 512, "proj_out", residual=x2)

    wr = _pad_cols(jnp.concatenate([moe_gr_w, moe_er_w], axis=1), LANES)
    br = _pad_cols(jnp.concatenate([moe_gr_b, moe_er_b]).reshape(1, -1), LANES)
    xn, eid, wgt = _router(x_mid, norm2_g, wr, br)
    y = _moe_experts(xn, eid[:, :TOP_K], moe_w1, moe_w3, moe_w2)
    return x_mid, y, wgt


def kernel(x, norm1_g, w_in, rw_mu, rw_w0, rw_w2, rw_a0, rw_a2, rw_g2, rw_k_k, rw_k_a, rw_r_k, rw_ln_w, rw_ln_b, dn_conv_w, dn_a_log, dn_dt_bias, dn_norm_w, gate_b, w_branch, w_out, norm2_g, moe_gr_w, moe_gr_b, moe_er_w, moe_er_b, moe_w1, moe_w3, moe_w2, final_g):
    batch, seq, d = x.shape
    assert norm1_g.shape[0] == 1, "single-layer block"
    x_mid, y, wgt = _layer(
        x.reshape(batch * seq, d), batch, seq, norm1_g[0], w_in[0], rw_mu[0], rw_w0[0], rw_w2[0], rw_a0[0],
        rw_a2[0], rw_g2[0], rw_k_k[0], rw_k_a[0], rw_r_k[0], rw_ln_w[0], rw_ln_b[0], dn_conv_w[0], dn_a_log[0],
        dn_dt_bias[0], dn_norm_w[0], gate_b[0], w_branch[0], w_out[0], norm2_g[0], moe_gr_w[0], moe_gr_b[0],
        moe_er_w[0], moe_er_b[0], moe_w1[0], moe_w3[0], moe_w2[0])
    return _final(x_mid, y, wgt, final_g).reshape(batch, seq, d)
```

```python
import functools

import jax
import jax.numpy as jnp
from jax import lax
from jax.experimental import pallas as pl
from jax.experimental.pallas import tpu as pltpu

F32 = jnp.float32
BF16 = jnp.bfloat16

D_MODEL = 4096
RW_HEADS = 32
RW_HEAD = 64
RW_WIDTH = RW_HEADS * RW_HEAD
RW_DECAY_LORA = 96
RW_A_LORA = 96
RW_GATE_LORA = 256
RW_GN_EPS = 64e-5
RW_DECAY_SCALE = 0.606531
DN_HEADS = 16
DN_HEAD = 128
DN_KEY = DN_HEADS * DN_HEAD
DN_VAL = DN_HEADS * DN_HEAD
DN_QKV = 2 * DN_KEY + DN_VAL
DN_CONV = 4
N_GROUPS = 8
EXPERTS_PER_GROUP = 8
N_EXPERTS = N_GROUPS * EXPERTS_PER_GROUP
TOP_K = 2
D_EXPERT = 768
NORM_EPS = 1e-6

LANES = 128
HALO_ROWS = 8
CHUNK = 64
LORA_PAD = 128
RW_PCOLS = 3 * RW_WIDTH + 2 * LORA_PAD + RW_GATE_LORA
DN_PCOLS = DN_QKV + LORA_PAD + DN_VAL
RW_OFF_R = 0
RW_OFF_WD = RW_WIDTH
RW_OFF_K = RW_OFF_WD + LORA_PAD
RW_OFF_V = RW_OFF_K + RW_WIDTH
RW_OFF_AD = RW_OFF_V + RW_WIDTH
RW_OFF_GD = RW_OFF_AD + LORA_PAD
DN_OFF_SMALL = DN_QKV
DN_OFF_Z = DN_QKV + LORA_PAD
MOE_ROWS = 256
UP_TILES = 2
UP_TILE = D_EXPERT // UP_TILES
VMEM_LIMIT = 56 * 1024 * 1024


def _params(n_axes):
    return pltpu.CompilerParams(dimension_semantics=("arbitrary",) * n_axes,
                                vmem_limit_bytes=VMEM_LIMIT)


def _dot(a, b):
    return jnp.dot(a.astype(BF16), b.astype(BF16), preferred_element_type=F32)


def _dot_nt(a, b):
    return lax.dot_general(a.astype(BF16), b.astype(BF16), (((1,), (1,)), ((), ())),
                           preferred_element_type=F32)


def _split2(x):
    hi = x.astype(BF16)
    lo = (x - hi.astype(F32)).astype(BF16)
    return hi, lo


def _sigmoid(x):
    return 1.0 / (1.0 + jnp.exp(-x))


def _silu(x):
    return x * _sigmoid(x)


def _rmsnorm_kernel(x_ref, g_ref, o_ref):
    x = x_ref[...]
    ms = jnp.mean(x * x, axis=-1, keepdims=True)
    o_ref[...] = (x * lax.rsqrt(ms + NORM_EPS) * g_ref[...]).astype(o_ref.dtype)


def _rmsnorm(x, g, out_dtype, tm=256):
    t, d = x.shape
    tm = min(tm, t)
    return pl.pallas_call(
        _rmsnorm_kernel,
        out_shape=jax.ShapeDtypeStruct((t, d), out_dtype),
        grid=(t // tm,),
        in_specs=[pl.BlockSpec((tm, d), lambda i: (i, 0)), pl.BlockSpec((1, d), lambda i: (0, 0))],
        out_specs=pl.BlockSpec((tm, d), lambda i: (i, 0)),
        compiler_params=_params(1),
        name="rmsnorm",
    )(x, g.reshape(1, d))


def _mm_kernel(a_ref, b_ref, o_ref):
    o_ref[...] = jnp.dot(a_ref[...], b_ref[...], preferred_element_type=F32).astype(o_ref.dtype)


def _mm_gate_kernel(a_ref, b_ref, bias_ref, o_ref):
    acc = jnp.dot(a_ref[...], b_ref[...], preferred_element_type=F32)
    o_ref[...] = _sigmoid(acc + bias_ref[...]).astype(o_ref.dtype)


def _mm_res_kernel(a_ref, b_ref, x_ref, o_ref):
    o_ref[...] = x_ref[...] + jnp.dot(a_ref[...], b_ref[...], preferred_element_type=F32)


def _matmul(a, b, out_dtype, tm, tn, name, bias=None, residual=None):
    m, k = a.shape
    n = b.shape[1]
    tm = min(tm, m)
    in_specs = [pl.BlockSpec((tm, k), lambda i, j: (i, 0)), pl.BlockSpec((k, tn), lambda i, j: (0, j))]
    args = [a, b]
    kern = _mm_kernel
    if bias is not None:
        in_specs.append(pl.BlockSpec((1, tn), lambda i, j: (0, j)))
        args.append(bias.reshape(1, n))
        kern = _mm_gate_kernel
    if residual is not None:
        in_specs.append(pl.BlockSpec((tm, tn), lambda i, j: (i, j)))
        args.append(residual)
        kern = _mm_res_kernel
    return pl.pallas_call(
        kern,
        out_shape=jax.ShapeDtypeStruct((m, n), out_dtype),
        grid=(m // tm, n // tn),
        in_specs=in_specs,
        out_specs=pl.BlockSpec((tm, tn), lambda i, j: (i, j)),
        compiler_params=_params(2),
        name=name,
    )(*args)


def _mix_kernel(a1_ref, b1_ref, a2_ref, b2_ref, g1_ref, g2_ref, o_ref):
    y1 = jnp.dot(a1_ref[...], b1_ref[...], preferred_element_type=F32)
    y2 = jnp.dot(a2_ref[...], b2_ref[...], preferred_element_type=F32)
    o_ref[...] = (g1_ref[...].astype(F32) * y1 + g2_ref[...].astype(F32) * y2).astype(o_ref.dtype)


def _branch_mix(o_rw, o_dn, wb_rw, wb_dn, gates, tm=1024, tn=512):
    m, k = o_rw.shape
    n = wb_rw.shape[1]
    tm = min(tm, m)
    goff = n // tn
    return pl.pallas_call(
        _mix_kernel,
        out_shape=jax.ShapeDtypeStruct((m, n), BF16),
        grid=(m // tm, n // tn),
        in_specs=[pl.BlockSpec((tm, k), lambda i, j: (i, 0)), pl.BlockSpec((k, tn), lambda i, j: (0, j)),
                  pl.BlockSpec((tm, k), lambda i, j: (i, 0)), pl.BlockSpec((k, tn), lambda i, j: (0, j)),
                  pl.BlockSpec((tm, tn), lambda i, j: (i, j)),
                  pl.BlockSpec((tm, tn), lambda i, j: (i, j + goff))],
        out_specs=pl.BlockSpec((tm, tn), lambda i, j: (i, j)),
        compiler_params=_params(2),
        name="branch_mix",
    )(o_rw, wb_rw, o_dn, wb_dn, gates, gates)


def _tri_incl():
    r = lax.broadcasted_iota(jnp.int32, (CHUNK, CHUNK), 0)
    c = lax.broadcasted_iota(jnp.int32, (CHUNK, CHUNK), 1)
    return (r >= c).astype(BF16)


def _eye_bf16(n):
    r = lax.broadcasted_iota(jnp.int32, (n, n), 0)
    c = lax.broadcasted_iota(jnp.int32, (n, n), 1)
    return (r == c).astype(BF16)


def _cumsum_rows(x):
    tri = _tri_incl()
    hi, lo = _split2(x)
    return (jnp.dot(tri, hi, preferred_element_type=F32) + jnp.dot(tri, lo, preferred_element_type=F32))


def _gdn_kernel(p_ref, halo_ref, cw_ref, alog_ref, dtb_ref, nw_ref, o_ref, ext_ref, state_ref):
    s = pl.program_id(1)

    @pl.when(s == 0)
    def _():
        state_ref[...] = jnp.zeros_like(state_ref)

    ext_ref[0:HALO_ROWS, :] = jnp.where(s == 0, 0.0, halo_ref[:, 0:DN_QKV])
    ext_ref[HALO_ROWS:HALO_ROWS + CHUNK, :] = p_ref[:, 0:DN_QKV]

    def conv_silu(col):
        cs = slice(col, col + DN_HEAD)
        acc = ext_ref[HALO_ROWS:HALO_ROWS + CHUNK, cs] * cw_ref[DN_CONV - 1:DN_CONV, cs]
        for j in range(DN_CONV - 1):
            r0 = HALO_ROWS - (DN_CONV - 1) + j
            acc = acc + ext_ref[r0:r0 + CHUNK, cs] * cw_ref[j:j + 1, cs]
        return _silu(acc)

    def l2norm(t):
        return t * lax.rsqrt(jnp.sum(t * t, axis=-1, keepdims=True) + NORM_EPS)

    small = p_ref[:, DN_OFF_SMALL:DN_OFF_SMALL + LORA_PAD]
    beta_all = _sigmoid(small)
    sp_in = small + dtb_ref[...]
    softplus = jnp.maximum(sp_in, 0.0) + jnp.log1p(jnp.exp(-jnp.abs(sp_in)))
    g_all = -jnp.exp(alog_ref[...]) * softplus
    gcum = _cumsum_rows(g_all)
    eye = _eye_bf16(LANES)
    ghi, glo = _split2(gcum)
    gcum_t = (lax.dot_general(eye, ghi, (((1,), (1,)), ((), ())), preferred_element_type=F32)
              + lax.dot_general(eye, glo, (((1,), (1,)), ((), ())), preferred_element_type=F32))

    r = lax.broadcasted_iota(jnp.int32, (CHUNK, CHUNK), 0)
    c = lax.broadcasted_iota(jnp.int32, (CHUNK, CHUNK), 1)
    causal = r >= c
    strict = r > c
    eye_c = jnp.where(r == c, 1.0, 0.0)
    nw = nw_ref[...]
    heads = range(DN_HEADS)

    q = [l2norm(conv_silu(h * DN_HEAD)) * (DN_HEAD ** -0.5) for h in heads]
    k = [l2norm(conv_silu(DN_KEY + h * DN_HEAD)) for h in heads]
    v = [conv_silu(2 * DN_KEY + h * DN_HEAD) for h in heads]
    gcol = [gcum[:, DN_HEADS + h:DN_HEADS + h + 1] for h in heads]
    glast = [g[CHUNK - 1:CHUNK, :] for g in gcol]
    decay = [jnp.where(causal, jnp.exp(jnp.where(causal, gcol[h] - gcum_t[DN_HEADS + h:DN_HEADS + h + 1, :], 0.0)),
                       0.0) for h in heads]
    beta = [beta_all[:, h:h + 1] for h in heads]
    kb = [k[h] * beta[h] for h in heads]
    eg = [jnp.exp(g) for g in gcol]
    kk = [_dot_nt(jnp.concatenate([kb[h], q[h]], axis=0), k[h]) for h in heads]
    k_tail_t = [_dot_nt(eye, k[h] * jnp.exp(glast[h] - gcol[h])) for h in heads]
    attn = [kk[h][CHUNK:] * decay[h] for h in heads]
    m = [-jnp.where(strict, kk[h][:CHUNK] * decay[h], 0.0) for h in heads]
    inv = [eye_c + m[h] for h in heads]
    m = [_dot(m[h], m[h]) for h in heads]
    for _ in range(4):
        both = [_dot(jnp.concatenate([m[h], inv[h]], axis=0), m[h]) for h in heads]
        inv = [inv[h] + both[h][CHUNK:] for h in heads]
        m = [both[h][:CHUNK] for h in heads]
    inv = [inv[h] + _dot(inv[h], m[h]) for h in heads]
    sol = [_dot(inv[h], jnp.concatenate([v[h] * beta[h], kb[h] * eg[h]], axis=1)) for h in heads]
    st = [state_ref[h] for h in heads]
    wq = [_dot(jnp.concatenate([sol[h][:, DN_HEAD:], q[h] * eg[h]], axis=0), st[h]) for h in heads]
    v_new = [sol[h][:, :DN_HEAD] - wq[h][:CHUNK] for h in heads]
    o = [wq[h][CHUNK:] + _dot(attn[h], v_new[h]) for h in heads]
    for h in heads:
        state_ref[h] = st[h] * jnp.exp(glast[h]) + _dot(k_tail_t[h], v_new[h])
    for h in heads:
        on = o[h] * lax.rsqrt(jnp.mean(o[h] * o[h], axis=-1, keepdims=True) + NORM_EPS) * nw
        z = p_ref[:, DN_OFF_Z + h * DN_HEAD:DN_OFF_Z + (h + 1) * DN_HEAD]
        o_ref[:, h * DN_HEAD:(h + 1) * DN_HEAD] = (on * _silu(z)).astype(o_ref.dtype)


def _gated_deltanet(p_dn, batch, seq, conv_w, a_log, dt_bias, norm_w):
    t = batch * seq
    ns = seq // CHUNK
    hb = CHUNK // HALO_ROWS
    alog = jnp.zeros((1, LORA_PAD), F32).at[0, DN_HEADS:2 * DN_HEADS].set(a_log)
    dtb = jnp.zeros((1, LORA_PAD), F32).at[0, DN_HEADS:2 * DN_HEADS].set(dt_bias)
    return pl.pallas_call(
        _gdn_kernel,
        out_shape=jax.ShapeDtypeStruct((t, DN_VAL), BF16),
        grid=(batch, ns),
        in_specs=[pl.BlockSpec((CHUNK, DN_PCOLS), lambda b, s: (b * ns + s, 0)),
                  pl.BlockSpec((HALO_ROWS, DN_PCOLS), lambda b, s: (jnp.maximum((b * ns + s) * hb - 1, 0), 0)),
                  pl.BlockSpec((DN_CONV, DN_QKV), lambda b, s: (0, 0)),
                  pl.BlockSpec((1, LORA_PAD), lambda b, s: (0, 0)),
                  pl.BlockSpec((1, LORA_PAD), lambda b, s: (0, 0)),
                  pl.BlockSpec((1, DN_HEAD), lambda b, s: (0, 0))],
        out_specs=pl.BlockSpec((CHUNK, DN_VAL), lambda b, s: (b * ns + s, 0)),
        scratch_shapes=[pltpu.VMEM((HALO_ROWS + CHUNK, DN_QKV), F32),
                        pltpu.VMEM((DN_HEADS, DN_HEAD, DN_HEAD), F32)],
        compiler_params=_params(2),
        name="gated_deltanet",
    )(p_dn, p_dn, conv_w, alog, dtb, norm_w.reshape(1, DN_HEAD))


def _rwkv_kernel(p_ref, halo_ref, mu_ref, w0_ref, a0_ref, kk_ref, ka_ref, rk_ref, lnw_ref, lnb_ref,
                 w2_ref, a2_ref, g2_ref, o_ref, state_ref):
    s = pl.program_id(1)

    @pl.when(s == 0)
    def _():
        state_ref[...] = jnp.zeros_like(state_ref)

    row1 = lax.broadcasted_iota(jnp.int32, (CHUNK, 1), 0)

    def shifted(lo, hi):
        cur = p_ref[:, lo:hi]
        last = jnp.where(s == 0, 0.0, halo_ref[HALO_ROWS - 1:HALO_ROWS, lo:hi])
        prev = jnp.where(row1 == 0, last, pltpu.roll(cur, 1, 0))
        return cur + (prev - cur) * mu_ref[:, lo:hi]

    wd = shifted(RW_OFF_WD, RW_OFF_WD + LORA_PAD)
    ad = shifted(RW_OFF_AD, RW_OFF_AD + LORA_PAD)
    gd = shifted(RW_OFF_GD, RW_OFF_GD + RW_GATE_LORA)
    log_w = -RW_DECAY_SCALE * _sigmoid(w0_ref[...] + _dot(jnp.tanh(wd), w2_ref[...]))
    a_all = _sigmoid(a0_ref[...] + _dot(ad, a2_ref[...]))
    g_all = _dot(_sigmoid(gd), g2_ref[...])

    lane = lax.broadcasted_iota(jnp.int32, (CHUNK, LANES), 1)
    row = lax.broadcasted_iota(jnp.int32, (CHUNK, LANES), 0)
    tcol = lane % RW_HEAD
    strict_ss = row > tcol
    incl_ss = row >= tcol
    eye_ss = jnp.where(row == tcol, 1.0, 0.0)
    r2 = lax.broadcasted_iota(jnp.int32, (LANES, LANES), 0)
    c2 = lax.broadcasted_iota(jnp.int32, (LANES, LANES), 1)
    bd_mask = (r2 // RW_HEAD) == (c2 // RW_HEAD)
    ones_bd = bd_mask.astype(BF16)
    head0 = lane < RW_HEAD
    eye = _eye_bf16(LANES)

    def seg_sum(x):
        s0 = jnp.sum(jnp.where(head0, x, 0.0), axis=-1, keepdims=True)
        s1 = jnp.sum(jnp.where(head0, 0.0, x), axis=-1, keepdims=True)
        return jnp.where(head0, s0, s1)

    def block_diag(x):
        return jnp.where(bd_mask, jnp.concatenate([x, x], axis=0), 0.0)

    def per_head_rows(x):
        return jnp.concatenate([jnp.where(head0, x, 0.0), jnp.where(head0, 0.0, x)], axis=0)

    pairs = range(RW_HEADS // 2)
    sl = [slice(p * LANES, (p + 1) * LANES) for p in pairs]
    r = [shifted(RW_OFF_R + p * LANES, RW_OFF_R + (p + 1) * LANES) for p in pairs]
    k = [shifted(RW_OFF_K + p * LANES, RW_OFF_K + (p + 1) * LANES) for p in pairs]
    v = [shifted(RW_OFF_V + p * LANES, RW_OFF_V + (p + 1) * LANES) for p in pairs]
    lw = [log_w[:, sl[p]] for p in pairs]
    a = [a_all[:, sl[p]] for p in pairs]
    kkr = [k[p] * kk_ref[:, sl[p]] for p in pairs]
    kss = [seg_sum(kkr[p] * kkr[p]) for p in pairs]
    gc = [_cumsum_rows(lw[p]) for p in pairs]
    kk = [kkr[p] * lax.rsqrt(kss[p] + NORM_EPS) for p in pairs]
    kh = [k[p] * (1.0 + (a[p] - 1.0) * ka_ref[:, sl[p]]) for p in pairs]
    b = [kk[p] * a[p] for p in pairs]
    bonus_s = [seg_sum(r[p] * kh[p] * rk_ref[:, sl[p]]) for p in pairs]
    glast = [g[CHUNK - 1:CHUNK, :] for g in gc]
    e_neg = [jnp.exp(-g) for g in gc]
    e_tail = [jnp.exp(glast[p] - gc[p]) for p in pairs]
    x1 = [jnp.concatenate([kk[p] * jnp.exp(gc[p] - lw[p]), r[p] * jnp.exp(gc[p])], axis=0) for p in pairs]
    x3 = [jnp.concatenate([kh[p] * e_tail[p], b[p] * e_tail[p]], axis=0) for p in pairs]
    st = [state_ref[p] for p in pairs]
    akb = [_dot_nt(x1[p], jnp.concatenate([per_head_rows(kh[p] * e_neg[p]), per_head_rows(b[p] * e_neg[p]),
                                           st[p]], axis=0)) for p in pairs]
    ak = [t[:, 0:LANES] for t in akb]
    ab = [t[:, LANES:2 * LANES] for t in akb]
    xh = [t[:, 2 * LANES:3 * LANES] for t in akb]
    bd_v = [block_diag(v[p]) for p in pairs]
    rhs = [xh[p][:CHUNK] + _dot(jnp.where(strict_ss, ak[p][:CHUNK], 0.0), bd_v[p]) for p in pairs]
    m = [-jnp.where(strict_ss, ab[p][:CHUNK], 0.0) for p in pairs]
    inv = [eye_ss + m[p] for p in pairs]
    m = [_dot(m[p], block_diag(m[p])) for p in pairs]
    for _ in range(4):
        both = [_dot(jnp.concatenate([m[p], inv[p]], axis=0), block_diag(m[p])) for p in pairs]
        inv = [inv[p] + both[p][CHUNK:] for p in pairs]
        m = [both[p][:CHUNK] for p in pairs]
    inv = [inv[p] + _dot(inv[p], block_diag(m[p])) for p in pairs]
    u = [_dot(inv[p], block_diag(rhs[p])) for p in pairs]
    o = [xh[p][CHUNK:] + _dot(jnp.concatenate([jnp.where(incl_ss, ak[p][CHUNK:], 0.0),
                                               jnp.where(incl_ss, ab[p][CHUNK:], 0.0)], axis=1),
                              jnp.concatenate([bd_v[p], -block_diag(u[p])], axis=0)) for p in pairs]
    vu_t = [jnp.concatenate([v[p], -u[p]], axis=0).T for p in pairs]
    for p in pairs:
        state_ref[p] = st[p] * jnp.exp(glast[p]) + jnp.where(bd_mask, _dot(vu_t[p], x3[p]), 0.0)
    oc = [o[p] - seg_sum(o[p]) * (1.0 / RW_HEAD) for p in pairs]
    var = [seg_sum(oc[p] * oc[p]) * (1.0 / RW_HEAD) for p in pairs]
    for p in pairs:
        on = oc[p] * lax.rsqrt(var[p] + RW_GN_EPS) * lnw_ref[:, sl[p]] + lnb_ref[:, sl[p]]
        o_ref[:, sl[p]] = ((on + bonus_s[p] * v[p]) * g_all[:, sl[p]]).astype(o_ref.dtype)


def _rwkv7(p_rw, batch, seq, mu, w0, w2, a0, a2, g2, k_k, k_a, r_k, ln_w, ln_b):
    t = batch * seq
    ns = seq // CHUNK
    hb = CHUNK // HALO_ROWS
    vec = lambda x: x.reshape(1, RW_WIDTH)
    cvec = pl.BlockSpec((1, RW_WIDTH), lambda b, s: (0, 0))
    return pl.pallas_call(
        _rwkv_kernel,
        out_shape=jax.ShapeDtypeStruct((t, RW_WIDTH), BF16),
        grid=(batch, ns),
        in_specs=[pl.BlockSpec((CHUNK, RW_PCOLS), lambda b, s: (b * ns + s, 0)),
                  pl.BlockSpec((HALO_ROWS, RW_PCOLS), lambda b, s: (jnp.maximum((b * ns + s) * hb - 1, 0), 0)),
                  pl.BlockSpec((1, RW_PCOLS), lambda b, s: (0, 0)),
                  cvec, cvec, cvec, cvec, cvec, cvec, cvec,
                  pl.BlockSpec((LORA_PAD, RW_WIDTH), lambda b, s: (0, 0)),
                  pl.BlockSpec((LORA_PAD, RW_WIDTH), lambda b, s: (0, 0)),
                  pl.BlockSpec((RW_GATE_LORA, RW_WIDTH), lambda b, s: (0, 0))],
        out_specs=pl.BlockSpec((CHUNK, RW_WIDTH), lambda b, s: (b * ns + s, 0)),
        scratch_shapes=[pltpu.VMEM((RW_HEADS // 2, LANES, LANES), F32)],
        compiler_params=_params(2),
        name="rwkv7",
    )(p_rw, p_rw, mu.reshape(1, RW_PCOLS), vec(w0), vec(a0), vec(k_k), vec(k_a), vec(r_k), vec(ln_w),
      vec(ln_b), w2, a2, g2)


PACK_ROWS = D_MODEL // (2 * LANES)
PACK_PITCH = PACK_ROWS + 2


def _pack_pair(a, b):
    ua = lax.bitcast_convert_type(a.astype(BF16).astype(F32), jnp.uint32)
    ub = lax.bitcast_convert_type(b.astype(BF16).astype(F32), jnp.uint32)
    return ua | lax.shift_right_logical(ub, jnp.uint32(16))


def _unpack_pair(u):
    a = lax.bitcast_convert_type(u & jnp.uint32(0xFFFF0000), F32)
    b = lax.bitcast_convert_type(lax.shift_left(u, jnp.uint32(16)), F32)
    return a, b


def _store_packed(ref, rows, x):
    half = D_MODEL // 2
    for j in range(PACK_ROWS):
        ref[pl.ds(j, rows, stride=PACK_PITCH), :] = _pack_pair(x[:, j * LANES:(j + 1) * LANES],
                                                               x[:, half + j * LANES:half + (j + 1) * LANES])
    for j in range(PACK_ROWS, PACK_PITCH):
        ref[pl.ds(j, rows, stride=PACK_PITCH), :] = jnp.zeros((rows, LANES), jnp.uint32)


def _router_kernel(x_ref, g_ref, wr_ref, br_ref, xpk_ref, eid_ref, wgt_ref):
    x = x_ref[...]
    xn = x * lax.rsqrt(jnp.mean(x * x, axis=-1, keepdims=True) + NORM_EPS) * g_ref[...]
    _store_packed(xpk_ref, x.shape[0], xn)
    xh, xl = _split2(xn)
    wh, wl = _split2(wr_ref[...])
    logits = (jnp.dot(xh, wh, preferred_element_type=F32) + jnp.dot(xh, wl, preferred_element_type=F32)
              + jnp.dot(xl, wh, preferred_element_type=F32)) + br_ref[...]
    tm = logits.shape[0]
    lane = lax.broadcasted_iota(jnp.int32, (tm, LANES), 1)
    neg = -1e30
    big = 4 * LANES

    def first_argmax(vals):
        mx = jnp.max(vals, axis=-1, keepdims=True)
        idx = jnp.min(jnp.where(vals == mx, lane, big), axis=-1, keepdims=True)
        return mx, idx

    gl = jnp.where(lane < N_GROUPS, logits, neg)
    gmax, gi = first_argmax(gl)
    gp = 1.0 / jnp.sum(jnp.where(lane < N_GROUPS, jnp.exp(gl - gmax), 0.0), axis=-1, keepdims=True)
    e_lo = N_GROUPS + gi * EXPERTS_PER_GROUP
    el = jnp.where((lane >= e_lo) & (lane < e_lo + EXPERTS_PER_GROUP), logits, neg)
    m1, i1 = first_argmax(el)
    m2, i2 = first_argmax(jnp.where(lane == i1, neg, el))
    e2 = jnp.exp(m2 - m1)
    w1 = gp / (1.0 + e2)
    w2 = gp * e2 / (1.0 + e2)
    eid_ref[...] = jnp.where(lane == 0, i1 - N_GROUPS, jnp.where(lane == 1, i2 - N_GROUPS, 0))
    wgt_ref[...] = jnp.where(lane == 0, w1, jnp.where(lane == 1, w2, 0.0))


def _router(x_mid, norm_g, wr, br, tm=256):
    t, d = x_mid.shape
    tm = min(tm, t)
    return pl.pallas_call(
        _router_kernel,
        out_shape=(jax.ShapeDtypeStruct((t * PACK_PITCH, LANES), jnp.uint32),
                   jax.ShapeDtypeStruct((t, LANES), jnp.int32),
                   jax.ShapeDtypeStruct((t, LANES), F32)),
        grid=(t // tm,),
        in_specs=[pl.BlockSpec((tm, d), lambda i: (i, 0)), pl.BlockSpec((1, d), lambda i: (0, 0)),
                  pl.BlockSpec((d, LANES), lambda i: (0, 0)), pl.BlockSpec((1, LANES), lambda i: (0, 0))],
        out_specs=(pl.BlockSpec((tm * PACK_PITCH, LANES), lambda i: (i, 0)),
                   pl.BlockSpec((tm, LANES), lambda i: (i, 0)), pl.BlockSpec((tm, LANES), lambda i: (i, 0))),
        compiler_params=_params(1),
        name="router",
    )(x_mid, norm_g.reshape(1, d), wr, br)


DMA_GROUP = 8


def _expert_changed(be_ref, i):
    return jnp.logical_or(i == 0, be_ref[i] != be_ref[jnp.maximum(i - 1, 0)])


def _moe_up_kernel(be_ref, nv_ref, row_ref, x_hbm, w1_ref, w3_ref, h_ref, xbuf, xb, wb1, wb3, sem):
    i = pl.program_id(1)
    nb = pl.num_programs(1)
    slot = i % 2
    half = D_MODEL // 2

    def start_gather(blk, sl):
        def group(g, carry):
            for j in range(DMA_GROUP):
                r = g * DMA_GROUP + j
                src = row_ref[blk * MOE_ROWS + r]
                pltpu.make_async_copy(x_hbm.at[pl.ds(src, PACK_ROWS)],
                                      xbuf.at[sl, pl.ds(r * PACK_PITCH, PACK_ROWS)], sem.at[sl]).start()
            return carry
        lax.fori_loop(0, MOE_ROWS // DMA_GROUP, group, 0)

    @pl.when(jnp.logical_and(i == 0, nv_ref[0] > 0))
    def _():
        start_gather(0, 0)

    nxt = jnp.minimum(i + 1, nb - 1)

    @pl.when(jnp.logical_and(i + 1 < nb, nv_ref[nxt] > 0))
    def _():
        start_gather(nxt, 1 - slot)

    @pl.when(nv_ref[i] > 0)
    def _():
        @pl.when(_expert_changed(be_ref, i))
        def _():
            wb1[...] = w1_ref[0].astype(BF16)
            wb3[...] = w3_ref[0].astype(BF16)

        n_rows = MOE_ROWS * PACK_ROWS
        pltpu.make_async_copy(x_hbm.at[pl.ds(0, n_rows)], xbuf.at[slot, pl.ds(0, n_rows)], sem.at[slot]).wait()
        xs = xbuf.at[slot]
        for j in range(PACK_ROWS):
            hi, lo = _unpack_pair(xs[pl.ds(j, MOE_ROWS, stride=PACK_PITCH), :])
            xb[:, j * LANES:(j + 1) * LANES] = hi.astype(BF16)
            xb[:, half + j * LANES:half + (j + 1) * LANES] = lo.astype(BF16)
        x = xb[...]
        h1 = jnp.dot(x, wb1[...], preferred_element_type=F32)
        h3 = jnp.dot(x, wb3[...], preferred_element_type=F32)
        h_ref[...] = (_silu(h1) * h3).astype(h_ref.dtype)

    @pl.when(nv_ref[i] == 0)
    def _():
        h_ref[...] = jnp.zeros_like(h_ref)


def _moe_down_kernel(be_ref, nv_ref, dst_ref, h_ref, w2_ref, y_hbm, ybuf, wb2, sem):
    i = pl.program_id(0)
    nb = pl.num_programs(0)
    slot = i % 2

    def row_copy(blk, sl, r):
        d = dst_ref[blk * MOE_ROWS + r]
        return pltpu.make_async_copy(ybuf.at[sl, pl.ds(r * PACK_PITCH, PACK_PITCH)],
                                     y_hbm.at[pl.ds(d, PACK_PITCH)], sem.at[sl])

    def scatter(blk, sl, wait):
        n = nv_ref[blk]
        ng = lax.shift_right_logical(n, 3)

        def group(g, carry):
            if wait:
                n_rows = DMA_GROUP * PACK_PITCH
                r0 = pl.multiple_of(g * n_rows, 8)
                pltpu.make_async_copy(ybuf.at[sl, pl.ds(r0, n_rows)], y_hbm.at[pl.ds(0, n_rows)],
                                      sem.at[sl]).wait()
            else:
                for j in range(DMA_GROUP):
                    row_copy(blk, sl, g * DMA_GROUP + j).start()
            return carry

        def single(r, carry):
            cp = row_copy(blk, sl, r)
            if wait:
                cp.wait()
            else:
                cp.start()
            return carry

        lax.fori_loop(0, ng, group, 0)
        lax.fori_loop(ng * DMA_GROUP, n, single, 0)

    prev2 = jnp.maximum(i - 2, 0)

    @pl.when(i >= 2)
    def _():
        scatter(prev2, slot, True)

    @pl.when(nv_ref[i] > 0)
    def _():
        @pl.when(_expert_changed(be_ref, i))
        def _():
            wb2[...] = w2_ref[0].astype(BF16)

        _store_packed(ybuf.at[slot], MOE_ROWS, jnp.dot(h_ref[...], wb2[...], preferred_element_type=F32))
        scatter(i, slot, False)

    prev1 = jnp.maximum(i - 1, 0)

    @pl.when(jnp.logical_and(i == nb - 1, i >= 1))
    def _():
        scatter(prev1, 1 - slot, True)

    @pl.when(i == nb - 1)
    def _():
        scatter(i, slot, True)


def _moe_experts(xpk, t, eid, w1, w3, w2):
    d = D_MODEL
    a = t * TOP_K
    e_flat = eid.reshape(a)
    tok_flat = jnp.repeat(jnp.arange(t, dtype=jnp.int32), TOP_K)
    dst_flat = (jnp.arange(a, dtype=jnp.int32) % TOP_K) * t + tok_flat
    onehot = (e_flat[:, None] == jnp.arange(N_EXPERTS, dtype=jnp.int32)[None, :]).astype(jnp.int32)
    csum = jnp.cumsum(onehot, axis=0)
    rank = jnp.sum(csum * onehot, axis=1) - 1
    counts = csum[-1]
    pcounts = (counts + MOE_ROWS - 1) // MOE_ROWS * MOE_ROWS
    pend = jnp.cumsum(pcounts)
    poffs = pend - pcounts
    dest = poffs[e_flat] + rank
    p_rows = a + N_EXPERTS * MOE_ROWS
    nb = p_rows // MOE_ROWS
    slot_dst = jnp.zeros((p_rows,), jnp.int32).at[dest].set(dst_flat)
    slot_src = (slot_dst % t) * PACK_PITCH
    slot_dst = slot_dst * PACK_PITCH
    starts = jnp.arange(nb, dtype=jnp.int32) * MOE_ROWS
    block_e = jnp.minimum(jnp.searchsorted(pend, starts, side='right'), N_EXPERTS - 1).astype(jnp.int32)
    block_v = jnp.clip(poffs[block_e] + counts[block_e] - starts, 0, MOE_ROWS).astype(jnp.int32)
    last_e = block_e[jnp.maximum(jnp.sum(starts < pend[-1]) - 1, 0)]
    block_e = jnp.where(starts < pend[-1], block_e, last_e)

    h = pl.pallas_call(
        _moe_up_kernel,
        out_shape=jax.ShapeDtypeStruct((p_rows, D_EXPERT), BF16),
        grid_spec=pltpu.PrefetchScalarGridSpec(
            num_scalar_prefetch=3,
            grid=(UP_TILES, nb),
            in_specs=[pl.BlockSpec(memory_space=pl.ANY),
                      pl.BlockSpec((1, d, UP_TILE), lambda j, i, be, bv, tk: (be[i], 0, j)),
                      pl.BlockSpec((1, d, UP_TILE), lambda j, i, be, bv, tk: (be[i], 0, j))],
            out_specs=pl.BlockSpec((MOE_ROWS, UP_TILE), lambda j, i, be, bv, tk: (i, j)),
            scratch_shapes=[pltpu.VMEM((2, MOE_ROWS * PACK_PITCH, LANES), jnp.uint32),
                            pltpu.VMEM((MOE_ROWS, d), BF16), pltpu.VMEM((d, UP_TILE), BF16),
                            pltpu.VMEM((d, UP_TILE), BF16), pltpu.SemaphoreType.DMA((2,))]),
        compiler_params=_params(2),
        name="moe_up",
    )(block_e, block_v, slot_src, xpk, w1, w3)

    y = pl.pallas_call(
        _moe_down_kernel,
        out_shape=jax.ShapeDtypeStruct((TOP_K * t * PACK_PITCH, LANES), jnp.uint32),
        grid_spec=pltpu.PrefetchScalarGridSpec(
            num_scalar_prefetch=3,
            grid=(nb,),
            in_specs=[pl.BlockSpec((MOE_ROWS, D_EXPERT), lambda i, be, bv, ds: (i, 0)),
                      pl.BlockSpec((1, D_EXPERT, d), lambda i, be, bv, ds: (be[i], 0, 0))],
            out_specs=pl.BlockSpec(memory_space=pl.ANY),
            scratch_shapes=[pltpu.VMEM((2, MOE_ROWS * PACK_PITCH, LANES), jnp.uint32),
                            pltpu.VMEM((D_EXPERT, d), BF16), pltpu.SemaphoreType.DMA((2,))]),
        compiler_params=_params(1),
        name="moe_down",
    )(block_e, block_v, slot_dst, h, w2)
    return y


def _final_kernel(x_ref, y0_ref, y1_ref, wgt_ref, g_ref, o_ref):
    tm = x_ref.shape[0]
    half = D_MODEL // 2
    w0 = wgt_ref[:, 0:1]
    w1 = wgt_ref[:, 1:2]
    ss = jnp.zeros((tm, 1), F32)
    for j in range(PACK_ROWS):
        a0, b0 = _unpack_pair(y0_ref[pl.ds(j, tm, stride=PACK_PITCH), :])
        a1, b1 = _unpack_pair(y1_ref[pl.ds(j, tm, stride=PACK_PITCH), :])
        ca = slice(j * LANES, (j + 1) * LANES)
        cb = slice(half + j * LANES, half + (j + 1) * LANES)
        za = x_ref[:, ca] + w0 * a0 + w1 * a1
        zb = x_ref[:, cb] + w0 * b0 + w1 * b1
        o_ref[:, ca] = za
        o_ref[:, cb] = zb
        ss = ss + jnp.sum(za * za, axis=-1, keepdims=True) + jnp.sum(zb * zb, axis=-1, keepdims=True)
    o_ref[...] = o_ref[...] * lax.rsqrt(ss * (1.0 / D_MODEL) + NORM_EPS) * g_ref[...]


def _final(x_mid, y, wgt, g, tm=256):
    t, d = x_mid.shape
    tm = min(tm, t)
    nblk = t // tm
    return pl.pallas_call(
        _final_kernel,
        out_shape=jax.ShapeDtypeStruct((t, d), F32),
        grid=(nblk,),
        in_specs=[pl.BlockSpec((tm, d), lambda i: (i, 0)),
                  pl.BlockSpec((tm * PACK_PITCH, LANES), lambda i: (i, 0)),
                  pl.BlockSpec((tm * PACK_PITCH, LANES), lambda i: (i + nblk, 0)),
                  pl.BlockSpec((tm, LANES), lambda i: (i, 0)), pl.BlockSpec((1, d), lambda i: (0, 0))],
        out_specs=pl.BlockSpec((tm, d), lambda i: (i, 0)),
        compiler_params=_params(1),
        name="final_combine",
    )(x_mid, y, y, wgt, g.reshape(1, d))


def _pad_cols(w, n):
    return jnp.pad(w, ((0, 0), (0, n - w.shape[1])))


def _pad_rows(w, n):
    return jnp.pad(w, ((0, n - w.shape[0]), (0, 0)))


def _layer(x2, batch, seq, norm1_g, w_in, rw_mu, rw_w0, rw_w2, rw_a0, rw_a2, rw_g2, rw_k_k, rw_k_a, rw_r_k,
           rw_ln_w, rw_ln_b, dn_conv_w, dn_a_log, dn_dt_bias, dn_norm_w, gate_b, w_branch, w_out, norm2_g,
           moe_gr_w, moe_gr_b, moe_er_w, moe_er_b, moe_w1, moe_w3, moe_w2):
    sizes = (RW_WIDTH, RW_DECAY_LORA, RW_WIDTH, RW_WIDTH, RW_A_LORA, RW_GATE_LORA,
             DN_QKV, 2 * DN_HEADS, DN_VAL, 2 * D_MODEL)
    offs = [0]
    for sz in sizes:
        offs.append(offs[-1] + sz)
    cols = lambda w, i: w[..., offs[i]:offs[i + 1]]
    w_bf = w_in.astype(BF16)
    w_rw = jnp.concatenate([cols(w_bf, 0), _pad_cols(cols(w_bf, 1), LORA_PAD), cols(w_bf, 2), cols(w_bf, 3),
                            _pad_cols(cols(w_bf, 4), LORA_PAD), cols(w_bf, 5)], axis=1)
    w_dn = jnp.concatenate([cols(w_bf, 6), _pad_cols(cols(w_bf, 7), LORA_PAD), cols(w_bf, 8)], axis=1)
    w_gate = cols(w_bf, 9)
    mu2 = rw_mu.reshape(1, -1)
    mu = jnp.concatenate([cols(mu2, 0), _pad_cols(cols(mu2, 1), LORA_PAD), cols(mu2, 2), cols(mu2, 3),
                          _pad_cols(cols(mu2, 4), LORA_PAD), cols(mu2, 5)], axis=1)

    h = _rmsnorm(x2, norm1_g, BF16)
    p_rw = _matmul(h, w_rw, F32, 1024, 512, "proj_rwkv")
    p_dn = _matmul(h, w_dn, F32, 1024, 640, "proj_gdn")
    gates = _matmul(h, w_gate, BF16, 1024, 512, "proj_gate", bias=gate_b)

    o_rw = _rwkv7(p_rw, batch, seq, mu, rw_w0, _pad_rows(rw_w2, LORA_PAD).astype(BF16), rw_a0,
                  _pad_rows(rw_a2, LORA_PAD).astype(BF16), rw_g2.astype(BF16), rw_k_k, rw_k_a,
                  rw_r_k.reshape(-1), rw_ln_w, rw_ln_b)
    o_dn = _gated_deltanet(p_dn, batch, seq, dn_conv_w, dn_a_log, dn_dt_bias, dn_norm_w)

    mix = _branch_mix(o_rw, o_dn, w_branch[:RW_WIDTH].astype(BF16), w_branch[RW_WIDTH:].astype(BF16), gates)
    x_mid = _matmul(mix, w_out.astype(BF16), F32, 1024, 512, "proj_out", residual=x2)

    wr = _pad_cols(jnp.concatenate([moe_gr_w, moe_er_w], axis=1), LANES)
    br = _pad_cols(jnp.concatenate([moe_gr_b, moe_er_b]).reshape(1, -1), LANES)
    xpk, eid, wgt = _router(x_mid, norm2_g, wr, br)
    y = _moe_experts(xpk, x_mid.shape[0], eid[:, :TOP_K], moe_w1, moe_w3, moe_w2)
    return x_mid, y, wgt


def kernel(x, norm1_g, w_in, rw_mu, rw_w0, rw_w2, rw_a0, rw_a2, rw_g2, rw_k_k, rw_k_a, rw_r_k, rw_ln_w, rw_ln_b, dn_conv_w, dn_a_log, dn_dt_bias, dn_norm_w, gate_b, w_branch, w_out, norm2_g, moe_gr_w, moe_gr_b, moe_er_w, moe_er_b, moe_w1, moe_w3, moe_w2, final_g):
    batch, seq, d = x.shape
    assert norm1_g.shape[0] == 1, "single-layer block"
    x_mid, y, wgt = _layer(
        x.reshape(batch * seq, d), batch, seq, norm1_g[0], w_in[0], rw_mu[0], rw_w0[0], rw_w2[0], rw_a0[0],
        rw_a2[0], rw_g2[0], rw_k_k[0], rw_k_a[0], rw_r_k[0], rw_ln_w[0], rw_ln_b[0], dn_conv_w[0], dn_a_log[0],
        dn_dt_bias[0], dn_norm_w[0], gate_b[0], w_branch[0], w_out[0], norm2_g[0], moe_gr_w[0], moe_gr_b[0],
        moe_er_w[0], moe_er_b[0], moe_w1[0], moe_w3[0], moe_w2[0])
    return _final(x_mid, y, wgt, final_g).reshape(batch, seq, d)
```

```python
import functools

import jax
import jax.numpy as jnp
from jax import lax
from jax.experimental import pallas as pl
from jax.experimental.pallas import tpu as pltpu

F32 = jnp.float32
BF16 = jnp.bfloat16

D_MODEL = 4096
RW_HEADS = 32
RW_HEAD = 64
RW_WIDTH = RW_HEADS * RW_HEAD
RW_DECAY_LORA = 96
RW_A_LORA = 96
RW_GATE_LORA = 256
RW_GN_EPS = 64e-5
RW_DECAY_SCALE = 0.606531
DN_HEADS = 16
DN_HEAD = 128
DN_KEY = DN_HEADS * DN_HEAD
DN_VAL = DN_HEADS * DN_HEAD
DN_QKV = 2 * DN_KEY + DN_VAL
DN_CONV = 4
N_GROUPS = 8
EXPERTS_PER_GROUP = 8
N_EXPERTS = N_GROUPS * EXPERTS_PER_GROUP
TOP_K = 2
D_EXPERT = 768
NORM_EPS = 1e-6

LANES = 128
HALO_ROWS = 8
CHUNK = 64
LORA_PAD = 128
RW_PCOLS = 3 * RW_WIDTH + 2 * LORA_PAD + RW_GATE_LORA
DN_PCOLS = DN_QKV + LORA_PAD + DN_VAL
RW_OFF_R = 0
RW_OFF_WD = RW_WIDTH
RW_OFF_K = RW_OFF_WD + LORA_PAD
RW_OFF_V = RW_OFF_K + RW_WIDTH
RW_OFF_AD = RW_OFF_V + RW_WIDTH
RW_OFF_GD = RW_OFF_AD + LORA_PAD
DN_OFF_SMALL = DN_QKV
DN_OFF_Z = DN_QKV + LORA_PAD
MOE_PAD = 512
UP_ROWS = 512
DOWN_ROWS = 256
UP_TILES = 2
UP_TILE = D_EXPERT // UP_TILES
VMEM_LIMIT = 56 * 1024 * 1024


def _params(n_axes):
    return pltpu.CompilerParams(dimension_semantics=("arbitrary",) * n_axes,
                                vmem_limit_bytes=VMEM_LIMIT)


def _dot(a, b):
    return jnp.dot(a.astype(BF16), b.astype(BF16), preferred_element_type=F32)


def _dot_nt(a, b):
    return lax.dot_general(a.astype(BF16), b.astype(BF16), (((1,), (1,)), ((), ())),
                           preferred_element_type=F32)


def _split2(x):
    hi = x.astype(BF16)
    lo = (x - hi.astype(F32)).astype(BF16)
    return hi, lo


def _sigmoid(x):
    return 1.0 / (1.0 + jnp.exp(-x))


def _silu(x):
    return x * _sigmoid(x)


def _rmsnorm_kernel(x_ref, g_ref, o_ref):
    x = x_ref[...]
    ms = jnp.mean(x * x, axis=-1, keepdims=True)
    o_ref[...] = (x * lax.rsqrt(ms + NORM_EPS) * g_ref[...]).astype(o_ref.dtype)


def _rmsnorm(x, g, out_dtype, tm=256):
    t, d = x.shape
    tm = min(tm, t)
    return pl.pallas_call(
        _rmsnorm_kernel,
        out_shape=jax.ShapeDtypeStruct((t, d), out_dtype),
        grid=(t // tm,),
        in_specs=[pl.BlockSpec((tm, d), lambda i: (i, 0)), pl.BlockSpec((1, d), lambda i: (0, 0))],
        out_specs=pl.BlockSpec((tm, d), lambda i: (i, 0)),
        compiler_params=_params(1),
        name="rmsnorm",
    )(x, g.reshape(1, d))


def _mm_kernel(a_ref, b_ref, o_ref):
    o_ref[...] = jnp.dot(a_ref[...], b_ref[...], preferred_element_type=F32).astype(o_ref.dtype)


def _mm_gate_kernel(a_ref, b_ref, bias_ref, o_ref):
    acc = jnp.dot(a_ref[...], b_ref[...], preferred_element_type=F32)
    o_ref[...] = _sigmoid(acc + bias_ref[...]).astype(o_ref.dtype)


def _mm_res_kernel(a_ref, b_ref, x_ref, o_ref):
    o_ref[...] = x_ref[...] + jnp.dot(a_ref[...], b_ref[...], preferred_element_type=F32)


def _matmul(a, b, out_dtype, tm, tn, name, bias=None, residual=None):
    m, k = a.shape
    n = b.shape[1]
    tm = min(tm, m)
    in_specs = [pl.BlockSpec((tm, k), lambda i, j: (i, 0)), pl.BlockSpec((k, tn), lambda i, j: (0, j))]
    args = [a, b]
    kern = _mm_kernel
    if bias is not None:
        in_specs.append(pl.BlockSpec((1, tn), lambda i, j: (0, j)))
        args.append(bias.reshape(1, n))
        kern = _mm_gate_kernel
    if residual is not None:
        in_specs.append(pl.BlockSpec((tm, tn), lambda i, j: (i, j)))
        args.append(residual)
        kern = _mm_res_kernel
    return pl.pallas_call(
        kern,
        out_shape=jax.ShapeDtypeStruct((m, n), out_dtype),
        grid=(m // tm, n // tn),
        in_specs=in_specs,
        out_specs=pl.BlockSpec((tm, tn), lambda i, j: (i, j)),
        compiler_params=_params(2),
        name=name,
    )(*args)


def _mix_kernel(a1_ref, b1_ref, a2_ref, b2_ref, g1_ref, g2_ref, o_ref):
    y1 = jnp.dot(a1_ref[...], b1_ref[...], preferred_element_type=F32)
    y2 = jnp.dot(a2_ref[...], b2_ref[...], preferred_element_type=F32)
    o_ref[...] = (g1_ref[...].astype(F32) * y1 + g2_ref[...].astype(F32) * y2).astype(o_ref.dtype)


def _branch_mix(o_rw, o_dn, wb_rw, wb_dn, gates, tm=1024, tn=512):
    m, k = o_rw.shape
    n = wb_rw.shape[1]
    tm = min(tm, m)
    goff = n // tn
    return pl.pallas_call(
        _mix_kernel,
        out_shape=jax.ShapeDtypeStruct((m, n), BF16),
        grid=(m // tm, n // tn),
        in_specs=[pl.BlockSpec((tm, k), lambda i, j: (i, 0)), pl.BlockSpec((k, tn), lambda i, j: (0, j)),
                  pl.BlockSpec((tm, k), lambda i, j: (i, 0)), pl.BlockSpec((k, tn), lambda i, j: (0, j)),
                  pl.BlockSpec((tm, tn), lambda i, j: (i, j)),
                  pl.BlockSpec((tm, tn), lambda i, j: (i, j + goff))],
        out_specs=pl.BlockSpec((tm, tn), lambda i, j: (i, j)),
        compiler_params=_params(2),
        name="branch_mix",
    )(o_rw, wb_rw, o_dn, wb_dn, gates, gates)


def _tri_incl():
    r = lax.broadcasted_iota(jnp.int32, (CHUNK, CHUNK), 0)
    c = lax.broadcasted_iota(jnp.int32, (CHUNK, CHUNK), 1)
    return (r >= c).astype(BF16)


def _eye_bf16(n):
    r = lax.broadcasted_iota(jnp.int32, (n, n), 0)
    c = lax.broadcasted_iota(jnp.int32, (n, n), 1)
    return (r == c).astype(BF16)


def _cumsum_rows(x):
    tri = _tri_incl()
    hi, lo = _split2(x)
    return (jnp.dot(tri, hi, preferred_element_type=F32) + jnp.dot(tri, lo, preferred_element_type=F32))


def _gdn_kernel(p_ref, halo_ref, cw_ref, alog_ref, dtb_ref, nw_ref, o_ref, ext_ref, state_ref):
    s = pl.program_id(1)

    @pl.when(s == 0)
    def _():
        state_ref[...] = jnp.zeros_like(state_ref)

    ext_ref[0:HALO_ROWS, :] = jnp.where(s == 0, 0.0, halo_ref[:, 0:DN_QKV])
    ext_ref[HALO_ROWS:HALO_ROWS + CHUNK, :] = p_ref[:, 0:DN_QKV]

    def conv_silu(col):
        cs = slice(col, col + DN_HEAD)
        acc = ext_ref[HALO_ROWS:HALO_ROWS + CHUNK, cs] * cw_ref[DN_CONV - 1:DN_CONV, cs]
        for j in range(DN_CONV - 1):
            r0 = HALO_ROWS - (DN_CONV - 1) + j
            acc = acc + ext_ref[r0:r0 + CHUNK, cs] * cw_ref[j:j + 1, cs]
        return _silu(acc)

    def l2norm(t):
        return t * lax.rsqrt(jnp.sum(t * t, axis=-1, keepdims=True) + NORM_EPS)

    small = p_ref[:, DN_OFF_SMALL:DN_OFF_SMALL + LORA_PAD]
    beta_all = _sigmoid(small)
    sp_in = small + dtb_ref[...]
    softplus = jnp.maximum(sp_in, 0.0) + jnp.log1p(jnp.exp(-jnp.abs(sp_in)))
    g_all = -jnp.exp(alog_ref[...]) * softplus
    gcum = _cumsum_rows(g_all)
    eye = _eye_bf16(LANES)
    ghi, glo = _split2(gcum)
    gcum_t = (lax.dot_general(eye, ghi, (((1,), (1,)), ((), ())), preferred_element_type=F32)
              + lax.dot_general(eye, glo, (((1,), (1,)), ((), ())), preferred_element_type=F32))

    r = lax.broadcasted_iota(jnp.int32, (CHUNK, CHUNK), 0)
    c = lax.broadcasted_iota(jnp.int32, (CHUNK, CHUNK), 1)
    causal = r >= c
    strict = r > c
    eye_c = jnp.where(r == c, 1.0, 0.0)
    nw = nw_ref[...]
    heads = range(DN_HEADS)

    q = [l2norm(conv_silu(h * DN_HEAD)) * (DN_HEAD ** -0.5) for h in heads]
    k = [l2norm(conv_silu(DN_KEY + h * DN_HEAD)) for h in heads]
    v = [conv_silu(2 * DN_KEY + h * DN_HEAD) for h in heads]
    gcol = [gcum[:, DN_HEADS + h:DN_HEADS + h + 1] for h in heads]
    glast = [g[CHUNK - 1:CHUNK, :] for g in gcol]
    decay = [jnp.where(causal, jnp.exp(jnp.where(causal, gcol[h] - gcum_t[DN_HEADS + h:DN_HEADS + h + 1, :], 0.0)),
                       0.0) for h in heads]
    beta = [beta_all[:, h:h + 1] for h in heads]
    kb = [k[h] * beta[h] for h in heads]
    eg = [jnp.exp(g) for g in gcol]
    kk = [_dot_nt(jnp.concatenate([kb[h], q[h]], axis=0), k[h]) for h in heads]
    k_tail_t = [_dot_nt(eye, k[h] * jnp.exp(glast[h] - gcol[h])) for h in heads]
    attn = [kk[h][CHUNK:] * decay[h] for h in heads]
    m = [-jnp.where(strict, kk[h][:CHUNK] * decay[h], 0.0) for h in heads]
    inv = [eye_c + m[h] for h in heads]
    m = [_dot(m[h], m[h]) for h in heads]
    for _ in range(4):
        both = [_dot(jnp.concatenate([m[h], inv[h]], axis=0), m[h]) for h in heads]
        inv = [inv[h] + both[h][CHUNK:] for h in heads]
        m = [both[h][:CHUNK] for h in heads]
    inv = [inv[h] + _dot(inv[h], m[h]) for h in heads]
    sol = [_dot(inv[h], jnp.concatenate([v[h] * beta[h], kb[h] * eg[h]], axis=1)) for h in heads]
    st = [state_ref[h] for h in heads]
    wq = [_dot(jnp.concatenate([sol[h][:, DN_HEAD:], q[h] * eg[h]], axis=0), st[h]) for h in heads]
    v_new = [sol[h][:, :DN_HEAD] - wq[h][:CHUNK] for h in heads]
    o = [wq[h][CHUNK:] + _dot(attn[h], v_new[h]) for h in heads]
    for h in heads:
        state_ref[h] = st[h] * jnp.exp(glast[h]) + _dot(k_tail_t[h], v_new[h])
    for h in heads:
        on = o[h] * lax.rsqrt(jnp.mean(o[h] * o[h], axis=-1, keepdims=True) + NORM_EPS) * nw
        z = p_ref[:, DN_OFF_Z + h * DN_HEAD:DN_OFF_Z + (h + 1) * DN_HEAD]
        o_ref[:, h * DN_HEAD:(h + 1) * DN_HEAD] = (on * _silu(z)).astype(o_ref.dtype)


def _gated_deltanet(p_dn, batch, seq, conv_w, a_log, dt_bias, norm_w):
    t = batch * seq
    ns = seq // CHUNK
    hb = CHUNK // HALO_ROWS
    alog = jnp.zeros((1, LORA_PAD), F32).at[0, DN_HEADS:2 * DN_HEADS].set(a_log)
    dtb = jnp.zeros((1, LORA_PAD), F32).at[0, DN_HEADS:2 * DN_HEADS].set(dt_bias)
    return pl.pallas_call(
        _gdn_kernel,
        out_shape=jax.ShapeDtypeStruct((t, DN_VAL), BF16),
        grid=(batch, ns),
        in_specs=[pl.BlockSpec((CHUNK, DN_PCOLS), lambda b, s: (b * ns + s, 0)),
                  pl.BlockSpec((HALO_ROWS, DN_PCOLS), lambda b, s: (jnp.maximum((b * ns + s) * hb - 1, 0), 0)),
                  pl.BlockSpec((DN_CONV, DN_QKV), lambda b, s: (0, 0)),
                  pl.BlockSpec((1, LORA_PAD), lambda b, s: (0, 0)),
                  pl.BlockSpec((1, LORA_PAD), lambda b, s: (0, 0)),
                  pl.BlockSpec((1, DN_HEAD), lambda b, s: (0, 0))],
        out_specs=pl.BlockSpec((CHUNK, DN_VAL), lambda b, s: (b * ns + s, 0)),
        scratch_shapes=[pltpu.VMEM((HALO_ROWS + CHUNK, DN_QKV), F32),
                        pltpu.VMEM((DN_HEADS, DN_HEAD, DN_HEAD), F32)],
        compiler_params=_params(2),
        name="gated_deltanet",
    )(p_dn, p_dn, conv_w, alog, dtb, norm_w.reshape(1, DN_HEAD))


def _rwkv_kernel(p_ref, halo_ref, mu_ref, w0_ref, a0_ref, kk_ref, ka_ref, rk_ref, lnw_ref, lnb_ref,
                 w2_ref, a2_ref, g2_ref, o_ref, state_ref):
    s = pl.program_id(1)

    @pl.when(s == 0)
    def _():
        state_ref[...] = jnp.zeros_like(state_ref)

    row1 = lax.broadcasted_iota(jnp.int32, (CHUNK, 1), 0)

    def shifted(lo, hi):
        cur = p_ref[:, lo:hi]
        last = jnp.where(s == 0, 0.0, halo_ref[HALO_ROWS - 1:HALO_ROWS, lo:hi])
        prev = jnp.where(row1 == 0, last, pltpu.roll(cur, 1, 0))
        return cur + (prev - cur) * mu_ref[:, lo:hi]

    wd = shifted(RW_OFF_WD, RW_OFF_WD + LORA_PAD)
    ad = shifted(RW_OFF_AD, RW_OFF_AD + LORA_PAD)
    gd = shifted(RW_OFF_GD, RW_OFF_GD + RW_GATE_LORA)
    log_w = -RW_DECAY_SCALE * _sigmoid(w0_ref[...] + _dot(jnp.tanh(wd), w2_ref[...]))
    a_all = _sigmoid(a0_ref[...] + _dot(ad, a2_ref[...]))
    g_all = _dot(_sigmoid(gd), g2_ref[...])

    lane = lax.broadcasted_iota(jnp.int32, (CHUNK, LANES), 1)
    row = lax.broadcasted_iota(jnp.int32, (CHUNK, LANES), 0)
    tcol = lane % RW_HEAD
    strict_ss = row > tcol
    incl_ss = row >= tcol
    eye_ss = jnp.where(row == tcol, 1.0, 0.0)
    r2 = lax.broadcasted_iota(jnp.int32, (LANES, LANES), 0)
    c2 = lax.broadcasted_iota(jnp.int32, (LANES, LANES), 1)
    bd_mask = (r2 // RW_HEAD) == (c2 // RW_HEAD)
    ones_bd = bd_mask.astype(BF16)
    head0 = lane < RW_HEAD
    eye = _eye_bf16(LANES)

    def seg_sum(x):
        s0 = jnp.sum(jnp.where(head0, x, 0.0), axis=-1, keepdims=True)
        s1 = jnp.sum(jnp.where(head0, 0.0, x), axis=-1, keepdims=True)
        return jnp.where(head0, s0, s1)

    def block_diag(x):
        return jnp.where(bd_mask, jnp.concatenate([x, x], axis=0), 0.0)

    def per_head_rows(x):
        return jnp.concatenate([jnp.where(head0, x, 0.0), jnp.where(head0, 0.0, x)], axis=0)

    pairs = range(RW_HEADS // 2)
    sl = [slice(p * LANES, (p + 1) * LANES) for p in pairs]
    r = [shifted(RW_OFF_R + p * LANES, RW_OFF_R + (p + 1) * LANES) for p in pairs]
    k = [shifted(RW_OFF_K + p * LANES, RW_OFF_K + (p + 1) * LANES) for p in pairs]
    v = [shifted(RW_OFF_V + p * LANES, RW_OFF_V + (p + 1) * LANES) for p in pairs]
    lw = [log_w[:, sl[p]] for p in pairs]
    a = [a_all[:, sl[p]] for p in pairs]
    kkr = [k[p] * kk_ref[:, sl[p]] for p in pairs]
    kss = [seg_sum(kkr[p] * kkr[p]) for p in pairs]
    gc = [_cumsum_rows(lw[p]) for p in pairs]
    kk = [kkr[p] * lax.rsqrt(kss[p] + NORM_EPS) for p in pairs]
    kh = [k[p] * (1.0 + (a[p] - 1.0) * ka_ref[:, sl[p]]) for p in pairs]
    b = [kk[p] * a[p] for p in pairs]
    bonus_s = [seg_sum(r[p] * kh[p] * rk_ref[:, sl[p]]) for p in pairs]
    glast = [g[CHUNK - 1:CHUNK, :] for g in gc]
    e_neg = [jnp.exp(-g) for g in gc]
    e_tail = [jnp.exp(glast[p] - gc[p]) for p in pairs]
    x1 = [jnp.concatenate([kk[p] * jnp.exp(gc[p] - lw[p]), r[p] * jnp.exp(gc[p])], axis=0) for p in pairs]
    x3 = [jnp.concatenate([kh[p] * e_tail[p], b[p] * e_tail[p]], axis=0) for p in pairs]
    st = [state_ref[p] for p in pairs]
    akb = [_dot_nt(x1[p], jnp.concatenate([per_head_rows(kh[p] * e_neg[p]), per_head_rows(b[p] * e_neg[p]),
                                           st[p]], axis=0)) for p in pairs]
    ak = [t[:, 0:LANES] for t in akb]
    ab = [t[:, LANES:2 * LANES] for t in akb]
    xh = [t[:, 2 * LANES:3 * LANES] for t in akb]
    bd_v = [block_diag(v[p]) for p in pairs]
    rhs = [xh[p][:CHUNK] + _dot(jnp.where(strict_ss, ak[p][:CHUNK], 0.0), bd_v[p]) for p in pairs]
    m = [-jnp.where(strict_ss, ab[p][:CHUNK], 0.0) for p in pairs]
    inv = [eye_ss + m[p] for p in pairs]
    m = [_dot(m[p], block_diag(m[p])) for p in pairs]
    for _ in range(4):
        both = [_dot(jnp.concatenate([m[p], inv[p]], axis=0), block_diag(m[p])) for p in pairs]
        inv = [inv[p] + both[p][CHUNK:] for p in pairs]
        m = [both[p][:CHUNK] for p in pairs]
    inv = [inv[p] + _dot(inv[p], block_diag(m[p])) for p in pairs]
    u = [_dot(inv[p], block_diag(rhs[p])) for p in pairs]
    o = [xh[p][CHUNK:] + _dot(jnp.concatenate([jnp.where(incl_ss, ak[p][CHUNK:], 0.0),
                                               jnp.where(incl_ss, ab[p][CHUNK:], 0.0)], axis=1),
                              jnp.concatenate([bd_v[p], -block_diag(u[p])], axis=0)) for p in pairs]
    vu_t = [jnp.concatenate([v[p], -u[p]], axis=0).T for p in pairs]
    for p in pairs:
        state_ref[p] = st[p] * jnp.exp(glast[p]) + jnp.where(bd_mask, _dot(vu_t[p], x3[p]), 0.0)
    oc = [o[p] - seg_sum(o[p]) * (1.0 / RW_HEAD) for p in pairs]
    var = [seg_sum(oc[p] * oc[p]) * (1.0 / RW_HEAD) for p in pairs]
    for p in pairs:
        on = oc[p] * lax.rsqrt(var[p] + RW_GN_EPS) * lnw_ref[:, sl[p]] + lnb_ref[:, sl[p]]
        o_ref[:, sl[p]] = ((on + bonus_s[p] * v[p]) * g_all[:, sl[p]]).astype(o_ref.dtype)


def _rwkv7(p_rw, batch, seq, mu, w0, w2, a0, a2, g2, k_k, k_a, r_k, ln_w, ln_b):
    t = batch * seq
    ns = seq // CHUNK
    hb = CHUNK // HALO_ROWS
    vec = lambda x: x.reshape(1, RW_WIDTH)
    cvec = pl.BlockSpec((1, RW_WIDTH), lambda b, s: (0, 0))
    return pl.pallas_call(
        _rwkv_kernel,
        out_shape=jax.ShapeDtypeStruct((t, RW_WIDTH), BF16),
        grid=(batch, ns),
        in_specs=[pl.BlockSpec((CHUNK, RW_PCOLS), lambda b, s: (b * ns + s, 0)),
                  pl.BlockSpec((HALO_ROWS, RW_PCOLS), lambda b, s: (jnp.maximum((b * ns + s) * hb - 1, 0), 0)),
                  pl.BlockSpec((1, RW_PCOLS), lambda b, s: (0, 0)),
                  cvec, cvec, cvec, cvec, cvec, cvec, cvec,
                  pl.BlockSpec((LORA_PAD, RW_WIDTH), lambda b, s: (0, 0)),
                  pl.BlockSpec((LORA_PAD, RW_WIDTH), lambda b, s: (0, 0)),
                  pl.BlockSpec((RW_GATE_LORA, RW_WIDTH), lambda b, s: (0, 0))],
        out_specs=pl.BlockSpec((CHUNK, RW_WIDTH), lambda b, s: (b * ns + s, 0)),
        scratch_shapes=[pltpu.VMEM((RW_HEADS // 2, LANES, LANES), F32)],
        compiler_params=_params(2),
        name="rwkv7",
    )(p_rw, p_rw, mu.reshape(1, RW_PCOLS), vec(w0), vec(a0), vec(k_k), vec(k_a), vec(r_k), vec(ln_w),
      vec(ln_b), w2, a2, g2)


PACK_ROWS = D_MODEL // (2 * LANES)
PACK_PITCH = PACK_ROWS + 2


def _pack_pair(a, b):
    ua = lax.bitcast_convert_type(a.astype(BF16).astype(F32), jnp.uint32)
    ub = lax.bitcast_convert_type(b.astype(BF16).astype(F32), jnp.uint32)
    return ua | lax.shift_right_logical(ub, jnp.uint32(16))


def _unpack_pair(u):
    a = lax.bitcast_convert_type(u & jnp.uint32(0xFFFF0000), F32)
    b = lax.bitcast_convert_type(lax.shift_left(u, jnp.uint32(16)), F32)
    return a, b


def _store_packed(ref, rows, x):
    half = D_MODEL // 2
    for j in range(PACK_ROWS):
        ref[pl.ds(j, rows, stride=PACK_PITCH), :] = _pack_pair(x[:, j * LANES:(j + 1) * LANES],
                                                               x[:, half + j * LANES:half + (j + 1) * LANES])
    for j in range(PACK_ROWS, PACK_PITCH):
        ref[pl.ds(j, rows, stride=PACK_PITCH), :] = jnp.zeros((rows, LANES), jnp.uint32)


def _router_kernel(x_ref, g_ref, wr_ref, br_ref, xpk_ref, eid_ref, wgt_ref):
    x = x_ref[...]
    xn = x * lax.rsqrt(jnp.mean(x * x, axis=-1, keepdims=True) + NORM_EPS) * g_ref[...]
    _store_packed(xpk_ref, x.shape[0], xn)
    xh, xl = _split2(xn)
    wh, wl = _split2(wr_ref[...])
    logits = (jnp.dot(xh, wh, preferred_element_type=F32) + jnp.dot(xh, wl, preferred_element_type=F32)
              + jnp.dot(xl, wh, preferred_element_type=F32)) + br_ref[...]
    tm = logits.shape[0]
    lane = lax.broadcasted_iota(jnp.int32, (tm, LANES), 1)
    neg = -1e30
    big = 4 * LANES

    def first_argmax(vals):
        mx = jnp.max(vals, axis=-1, keepdims=True)
        idx = jnp.min(jnp.where(vals == mx, lane, big), axis=-1, keepdims=True)
        return mx, idx

    gl = jnp.where(lane < N_GROUPS, logits, neg)
    gmax, gi = first_argmax(gl)
    gp = 1.0 / jnp.sum(jnp.where(lane < N_GROUPS, jnp.exp(gl - gmax), 0.0), axis=-1, keepdims=True)
    e_lo = N_GROUPS + gi * EXPERTS_PER_GROUP
    el = jnp.where((lane >= e_lo) & (lane < e_lo + EXPERTS_PER_GROUP), logits, neg)
    m1, i1 = first_argmax(el)
    m2, i2 = first_argmax(jnp.where(lane == i1, neg, el))
    e2 = jnp.exp(m2 - m1)
    w1 = gp / (1.0 + e2)
    w2 = gp * e2 / (1.0 + e2)
    eid_ref[...] = jnp.where(lane == 0, i1 - N_GROUPS, jnp.where(lane == 1, i2 - N_GROUPS, 0))
    wgt_ref[...] = jnp.where(lane == 0, w1, jnp.where(lane == 1, w2, 0.0))


def _router(x_mid, norm_g, wr, br, tm=256):
    t, d = x_mid.shape
    tm = min(tm, t)
    return pl.pallas_call(
        _router_kernel,
        out_shape=(jax.ShapeDtypeStruct((t * PACK_PITCH, LANES), jnp.uint32),
                   jax.ShapeDtypeStruct((t, LANES), jnp.int32),
                   jax.ShapeDtypeStruct((t, LANES), F32)),
        grid=(t // tm,),
        in_specs=[pl.BlockSpec((tm, d), lambda i: (i, 0)), pl.BlockSpec((1, d), lambda i: (0, 0)),
                  pl.BlockSpec((d, LANES), lambda i: (0, 0)), pl.BlockSpec((1, LANES), lambda i: (0, 0))],
        out_specs=(pl.BlockSpec((tm * PACK_PITCH, LANES), lambda i: (i, 0)),
                   pl.BlockSpec((tm, LANES), lambda i: (i, 0)), pl.BlockSpec((tm, LANES), lambda i: (i, 0))),
        compiler_params=_params(1),
        name="router",
    )(x_mid, norm_g.reshape(1, d), wr, br)


DMA_GROUP = 8


def _expert_changed(be_ref, i):
    return jnp.logical_or(i == 0, be_ref[i] != be_ref[jnp.maximum(i - 1, 0)])


def _moe_up_kernel(be_ref, nv_ref, row_ref, x_hbm, w1_ref, w3_ref, h_ref, xbuf, xb, wb1, wb3, sem):
    i = pl.program_id(1)
    nb = pl.num_programs(1)
    slot = i % 2
    half = D_MODEL // 2
    rows = h_ref.shape[0]

    def start_gather(blk, sl):
        def group(g, carry):
            for j in range(DMA_GROUP):
                r = g * DMA_GROUP + j
                src = row_ref[blk * rows + r]
                pltpu.make_async_copy(x_hbm.at[pl.ds(src, PACK_ROWS)],
                                      xbuf.at[sl, pl.ds(r * PACK_PITCH, PACK_ROWS)],
                                      sem.at[sl]).start(priority=j % 2)
            return carry
        lax.fori_loop(0, rows // DMA_GROUP, group, 0)

    @pl.when(jnp.logical_and(i == 0, nv_ref[0] > 0))
    def _():
        start_gather(0, 0)

    nxt = jnp.minimum(i + 1, nb - 1)

    @pl.when(jnp.logical_and(i + 1 < nb, nv_ref[nxt] > 0))
    def _():
        start_gather(nxt, 1 - slot)

    @pl.when(nv_ref[i] > 0)
    def _():
        @pl.when(_expert_changed(be_ref, i))
        def _():
            wb1[...] = w1_ref[0].astype(BF16)
            wb3[...] = w3_ref[0].astype(BF16)

        n_rows = rows * PACK_ROWS
        pltpu.make_async_copy(x_hbm.at[pl.ds(0, n_rows)], xbuf.at[slot, pl.ds(0, n_rows)], sem.at[slot]).wait()
        xs = xbuf.at[slot]
        for j in range(PACK_ROWS):
            hi, lo = _unpack_pair(xs[pl.ds(j, rows, stride=PACK_PITCH), :])
            xb[:, j * LANES:(j + 1) * LANES] = hi.astype(BF16)
            xb[:, half + j * LANES:half + (j + 1) * LANES] = lo.astype(BF16)
        x = xb[...]
        h1 = jnp.dot(x, wb1[...], preferred_element_type=F32)
        h3 = jnp.dot(x, wb3[...], preferred_element_type=F32)
        h_ref[...] = (_silu(h1) * h3).astype(h_ref.dtype)

    @pl.when(nv_ref[i] == 0)
    def _():
        h_ref[...] = jnp.zeros_like(h_ref)


def _moe_down_kernel(be_ref, nv_ref, dst_ref, h_ref, w2_ref, y_hbm, ybuf, wb2, sem):
    i = pl.program_id(0)
    nb = pl.num_programs(0)
    slot = i % 2
    rows = h_ref.shape[0]

    def row_copy(blk, sl, r):
        d = dst_ref[blk * rows + r]
        return pltpu.make_async_copy(ybuf.at[sl, pl.ds(r * PACK_PITCH, PACK_PITCH)],
                                     y_hbm.at[pl.ds(d, PACK_PITCH)], sem.at[sl])

    def scatter(blk, sl, wait):
        n = nv_ref[blk]
        ng = lax.shift_right_logical(n, 3)

        def group(g, carry):
            if wait:
                n_rows = DMA_GROUP * PACK_PITCH
                r0 = pl.multiple_of(g * n_rows, 8)
                pltpu.make_async_copy(ybuf.at[sl, pl.ds(r0, n_rows)], y_hbm.at[pl.ds(0, n_rows)],
                                      sem.at[sl]).wait()
            else:
                for j in range(DMA_GROUP):
                    row_copy(blk, sl, g * DMA_GROUP + j).start()
            return carry

        def single(r, carry):
            cp = row_copy(blk, sl, r)
            if wait:
                cp.wait()
            else:
                cp.start()
            return carry

        lax.fori_loop(0, ng, group, 0)
        lax.fori_loop(ng * DMA_GROUP, n, single, 0)

    prev2 = jnp.maximum(i - 2, 0)

    @pl.when(i >= 2)
    def _():
        scatter(prev2, slot, True)

    @pl.when(nv_ref[i] > 0)
    def _():
        @pl.when(_expert_changed(be_ref, i))
        def _():
            wb2[...] = w2_ref[0].astype(BF16)

        _store_packed(ybuf.at[slot], rows, jnp.dot(h_ref[...], wb2[...], preferred_element_type=F32))
        scatter(i, slot, False)

    prev1 = jnp.maximum(i - 1, 0)

    @pl.when(jnp.logical_and(i == nb - 1, i >= 1))
    def _():
        scatter(prev1, 1 - slot, True)

    @pl.when(i == nb - 1)
    def _():
        scatter(i, slot, True)


def _moe_experts(xpk, t, eid, w1, w3, w2):
    d = D_MODEL
    a = t * TOP_K
    e_flat = eid.reshape(a)
    tok_flat = jnp.repeat(jnp.arange(t, dtype=jnp.int32), TOP_K)
    dst_flat = (jnp.arange(a, dtype=jnp.int32) % TOP_K) * t + tok_flat
    onehot = (e_flat[:, None] == jnp.arange(N_EXPERTS, dtype=jnp.int32)[None, :]).astype(jnp.int32)
    csum = jnp.cumsum(onehot, axis=0)
    rank = jnp.sum(csum * onehot, axis=1) - 1
    counts = csum[-1]
    pcounts = (counts + MOE_PAD - 1) // MOE_PAD * MOE_PAD
    pend = jnp.cumsum(pcounts)
    poffs = pend - pcounts
    dest = poffs[e_flat] + rank
    p_rows = a + N_EXPERTS * MOE_PAD
    slot_dst = jnp.zeros((p_rows,), jnp.int32).at[dest].set(dst_flat)
    slot_src = (slot_dst % t) * PACK_PITCH
    slot_dst = slot_dst * PACK_PITCH

    def blocks(rows):
        starts = jnp.arange(p_rows // rows, dtype=jnp.int32) * rows
        block_e = jnp.minimum(jnp.searchsorted(pend, starts, side='right'), N_EXPERTS - 1).astype(jnp.int32)
        block_v = jnp.clip(poffs[block_e] + counts[block_e] - starts, 0, rows).astype(jnp.int32)
        last_e = block_e[jnp.maximum(jnp.sum(starts < pend[-1]) - 1, 0)]
        return jnp.where(starts < pend[-1], block_e, last_e), block_v

    up_e, up_v = blocks(UP_ROWS)
    h = pl.pallas_call(
        _moe_up_kernel,
        out_shape=jax.ShapeDtypeStruct((p_rows, D_EXPERT), BF16),
        grid_spec=pltpu.PrefetchScalarGridSpec(
            num_scalar_prefetch=3,
            grid=(UP_TILES, p_rows // UP_ROWS),
            in_specs=[pl.BlockSpec(memory_space=pl.ANY),
                      pl.BlockSpec((1, d, UP_TILE), lambda j, i, be, bv, tk: (be[i], 0, j)),
                      pl.BlockSpec((1, d, UP_TILE), lambda j, i, be, bv, tk: (be[i], 0, j))],
            out_specs=pl.BlockSpec((UP_ROWS, UP_TILE), lambda j, i, be, bv, tk: (i, j)),
            scratch_shapes=[pltpu.VMEM((2, UP_ROWS * PACK_PITCH, LANES), jnp.uint32),
                            pltpu.VMEM((UP_ROWS, d), BF16), pltpu.VMEM((d, UP_TILE), BF16),
                            pltpu.VMEM((d, UP_TILE), BF16), pltpu.SemaphoreType.DMA((2,))]),
        compiler_params=_params(2),
        name="moe_up",
    )(up_e, up_v, slot_src, xpk, w1, w3)

    down_e, down_v = blocks(DOWN_ROWS)
    y = pl.pallas_call(
        _moe_down_kernel,
        out_shape=jax.ShapeDtypeStruct((TOP_K * t * PACK_PITCH, LANES), jnp.uint32),
        grid_spec=pltpu.PrefetchScalarGridSpec(
            num_scalar_prefetch=3,
            grid=(p_rows // DOWN_ROWS,),
            in_specs=[pl.BlockSpec((DOWN_ROWS, D_EXPERT), lambda i, be, bv, ds: (i, 0)),
                      pl.BlockSpec((1, D_EXPERT, d), lambda i, be, bv, ds: (be[i], 0, 0))],
            out_specs=pl.BlockSpec(memory_space=pl.ANY),
            scratch_shapes=[pltpu.VMEM((2, DOWN_ROWS * PACK_PITCH, LANES), jnp.uint32),
                            pltpu.VMEM((D_EXPERT, d), BF16), pltpu.SemaphoreType.DMA((2,))]),
        compiler_params=_params(1),
        name="moe_down",
    )(down_e, down_v, slot_dst, h, w2)
    return y


def _final_kernel(x_ref, y0_ref, y1_ref, wgt_ref, g_ref, o_ref):
    tm = x_ref.shape[0]
    half = D_MODEL // 2
    w0 = wgt_ref[:, 0:1]
    w1 = wgt_ref[:, 1:2]
    ss = jnp.zeros((tm, 1), F32)
    for j in range(PACK_ROWS):
        a0, b0 = _unpack_pair(y0_ref[pl.ds(j, tm, stride=PACK_PITCH), :])
        a1, b1 = _unpack_pair(y1_ref[pl.ds(j, tm, stride=PACK_PITCH), :])
        ca = slice(j * LANES, (j + 1) * LANES)
        cb = slice(half + j * LANES, half + (j + 1) * LANES)
        za = x_ref[:, ca] + w0 * a0 + w1 * a1
        zb = x_ref[:, cb] + w0 * b0 + w1 * b1
        o_ref[:, ca] = za
        o_ref[:, cb] = zb
        ss = ss + jnp.sum(za * za, axis=-1, keepdims=True) + jnp.sum(zb * zb, axis=-1, keepdims=True)
    o_ref[...] = o_ref[...] * lax.rsqrt(ss * (1.0 / D_MODEL) + NORM_EPS) * g_ref[...]


def _final(x_mid, y, wgt, g, tm=256):
    t, d = x_mid.shape
    tm = min(tm, t)
    nblk = t // tm
    return pl.pallas_call(
        _final_kernel,
        out_shape=jax.ShapeDtypeStruct((t, d), F32),
        grid=(nblk,),
        in_specs=[pl.BlockSpec((tm, d), lambda i: (i, 0)),
                  pl.BlockSpec((tm * PACK_PITCH, LANES), lambda i: (i, 0)),
                  pl.BlockSpec((tm * PACK_PITCH, LANES), lambda i: (i + nblk, 0)),
                  pl.BlockSpec((tm, LANES), lambda i: (i, 0)), pl.BlockSpec((1, d), lambda i: (0, 0))],
        out_specs=pl.BlockSpec((tm, d), lambda i: (i, 0)),
        compiler_params=_params(1),
        name="final_combine",
    )(x_mid, y, y, wgt, g.reshape(1, d))


def _pad_cols(w, n):
    return jnp.pad(w, ((0, 0), (0, n - w.shape[1])))


def _pad_rows(w, n):
    return jnp.pad(w, ((0, n - w.shape[0]), (0, 0)))


def _layer(x2, batch, seq, norm1_g, w_in, rw_mu, rw_w0, rw_w2, rw_a0, rw_a2, rw_g2, rw_k_k, rw_k_a, rw_r_k,
           rw_ln_w, rw_ln_b, dn_conv_w, dn_a_log, dn_dt_bias, dn_norm_w, gate_b, w_branch, w_out, norm2_g,
           moe_gr_w, moe_gr_b, moe_er_w, moe_er_b, moe_w1, moe_w3, moe_w2):
    sizes = (RW_WIDTH, RW_DECAY_LORA, RW_WIDTH, RW_WIDTH, RW_A_LORA, RW_GATE_LORA,
             DN_QKV, 2 * DN_HEADS, DN_VAL, 2 * D_MODEL)
    offs = [0]
    for sz in sizes:
        offs.append(offs[-1] + sz)
    cols = lambda w, i: w[..., offs[i]:offs[i + 1]]
    w_bf = w_in.astype(BF16)
    w_rw = jnp.concatenate([cols(w_bf, 0), _pad_cols(cols(w_bf, 1), LORA_PAD), cols(w_bf, 2), cols(w_bf, 3),
                            _pad_cols(cols(w_bf, 4), LORA_PAD), cols(w_bf, 5)], axis=1)
    w_dn = jnp.concatenate([cols(w_bf, 6), _pad_cols(cols(w_bf, 7), LORA_PAD), cols(w_bf, 8)], axis=1)
    w_gate = cols(w_bf, 9)
    mu2 = rw_mu.reshape(1, -1)
    mu = jnp.concatenate([cols(mu2, 0), _pad_cols(cols(mu2, 1), LORA_PAD), cols(mu2, 2), cols(mu2, 3),
                          _pad_cols(cols(mu2, 4), LORA_PAD), cols(mu2, 5)], axis=1)

    h = _rmsnorm(x2, norm1_g, BF16)
    p_rw = _matmul(h, w_rw, F32, 1024, 512, "proj_rwkv")
    p_dn = _matmul(h, w_dn, F32, 1024, 640, "proj_gdn")
    gates = _matmul(h, w_gate, BF16, 1024, 512, "proj_gate", bias=gate_b)

    o_rw = _rwkv7(p_rw, batch, seq, mu, rw_w0, _pad_rows(rw_w2, LORA_PAD).astype(BF16), rw_a0,
                  _pad_rows(rw_a2, LORA_PAD).astype(BF16), rw_g2.astype(BF16), rw_k_k, rw_k_a,
                  rw_r_k.reshape(-1), rw_ln_w, rw_ln_b)
    o_dn = _gated_deltanet(p_dn, batch, seq, dn_conv_w, dn_a_log, dn_dt_bias, dn_norm_w)

    mix = _branch_mix(o_rw, o_dn, w_branch[:RW_WIDTH].astype(BF16), w_branch[RW_WIDTH:].astype(BF16), gates)
    x_mid = _matmul(mix, w_out.astype(BF16), F32, 1024, 512, "proj_out", residual=x2)

    wr = _pad_cols(jnp.concatenate([moe_gr_w, moe_er_w], axis=1), LANES)
    br = _pad_cols(jnp.concatenate([moe_gr_b, moe_er_b]).reshape(1, -1), LANES)
    xpk, eid, wgt = _router(x_mid, norm2_g, wr, br)
    y = _moe_experts(xpk, x_mid.shape[0], eid[:, :TOP_K], moe_w1, moe_w3, moe_w2)
    return x_mid, y, wgt


def kernel(x, norm1_g, w_in, rw_mu, rw_w0, rw_w2, rw_a0, rw_a2, rw_g2, rw_k_k, rw_k_a, rw_r_k, rw_ln_w, rw_ln_b, dn_conv_w, dn_a_log, dn_dt_bias, dn_norm_w, gate_b, w_branch, w_out, norm2_g, moe_gr_w, moe_gr_b, moe_er_w, moe_er_b, moe_w1, moe_w3, moe_w2, final_g):
    batch, seq, d = x.shape
    assert norm1_g.shape[0] == 1, "single-layer block"
    x_mid, y, wgt = _layer(
        x.reshape(batch * seq, d), batch, seq, norm1_g[0], w_in[0], rw_mu[0], rw_w0[0], rw_w2[0], rw_a0[0],
        rw_a2[0], rw_g2[0], rw_k_k[0], rw_k_a[0], rw_r_k[0], rw_ln_w[0], rw_ln_b[0], dn_conv_w[0], dn_a_log[0],
        dn_dt_bias[0], dn_norm_w[0], gate_b[0], w_branch[0], w_out[0], norm2_g[0], moe_gr_w[0], moe_gr_b[0],
        moe_er_w[0], moe_er_b[0], moe_w1[0], moe_w3[0], moe_w2[0])
    return _final(x_mid, y, wgt, final_g).reshape(batch, seq, d)
```

```python
import functools

import jax
import jax.numpy as jnp
from jax import lax
from jax.experimental import pallas as pl
from jax.experimental.pallas import tpu as pltpu

F32 = jnp.float32
BF16 = jnp.bfloat16

D_MODEL = 4096
RW_HEADS = 32
RW_HEAD = 64
RW_WIDTH = RW_HEADS * RW_HEAD
RW_DECAY_LORA = 96
RW_A_LORA = 96
RW_GATE_LORA = 256
RW_GN_EPS = 64e-5
RW_DECAY_SCALE = 0.606531
DN_HEADS = 16
DN_HEAD = 128
DN_KEY = DN_HEADS * DN_HEAD
DN_VAL = DN_HEADS * DN_HEAD
DN_QKV = 2 * DN_KEY + DN_VAL
DN_CONV = 4
N_GROUPS = 8
EXPERTS_PER_GROUP = 8
N_EXPERTS = N_GROUPS * EXPERTS_PER_GROUP
TOP_K = 2
D_EXPERT = 768
NORM_EPS = 1e-6

LANES = 128
HALO_ROWS = 8
CHUNK = 64
RW_PAIR_GROUP = 16
LORA_PAD = 128
RW_PCOLS = 3 * RW_WIDTH + 2 * LORA_PAD + RW_GATE_LORA
MXU_WIDTH = 256
DN_PCOLS = -(-(DN_QKV + LORA_PAD + DN_VAL) // (3 * MXU_WIDTH)) * (3 * MXU_WIDTH)
RW_OFF_R = 0
RW_OFF_WD = RW_WIDTH
RW_OFF_K = RW_OFF_WD + LORA_PAD
RW_OFF_V = RW_OFF_K + RW_WIDTH
RW_OFF_AD = RW_OFF_V + RW_WIDTH
RW_OFF_GD = RW_OFF_AD + LORA_PAD
DN_OFF_SMALL = DN_QKV
DN_OFF_Z = DN_QKV + LORA_PAD
MOE_PAD = 512
UP_ROWS = 512
DOWN_ROWS = 256
UP_TILES = 3
UP_TILE = D_EXPERT // UP_TILES
VMEM_LIMIT = 56 * 1024 * 1024


def _params(n_axes):
    return pltpu.CompilerParams(dimension_semantics=("arbitrary",) * n_axes,
                                vmem_limit_bytes=VMEM_LIMIT)


def _dot(a, b):
    return jnp.dot(a.astype(BF16), b.astype(BF16), preferred_element_type=F32)


def _dot_nt(a, b):
    return lax.dot_general(a.astype(BF16), b.astype(BF16), (((1,), (1,)), ((), ())),
                           preferred_element_type=F32)


def _split2(x):
    hi = x.astype(BF16)
    lo = (x - hi.astype(F32)).astype(BF16)
    return hi, lo


def _sigmoid(x):
    return 1.0 / (1.0 + jnp.exp(-x))


def _silu(x):
    return x * _sigmoid(x)


def _rmsnorm_kernel(x_ref, g_ref, o_ref):
    x = x_ref[...]
    ms = jnp.mean(x * x, axis=-1, keepdims=True)
    o_ref[...] = (x * lax.rsqrt(ms + NORM_EPS) * g_ref[...]).astype(o_ref.dtype)


def _rmsnorm(x, g, out_dtype, tm=256):
    t, d = x.shape
    tm = min(tm, t)
    return pl.pallas_call(
        _rmsnorm_kernel,
        out_shape=jax.ShapeDtypeStruct((t, d), out_dtype),
        grid=(t // tm,),
        in_specs=[pl.BlockSpec((tm, d), lambda i: (i, 0)), pl.BlockSpec((1, d), lambda i: (0, 0))],
        out_specs=pl.BlockSpec((tm, d), lambda i: (i, 0)),
        compiler_params=_params(1),
        name="rmsnorm",
    )(x, g.reshape(1, d))


def _mm_kernel(a_ref, b_ref, o_ref):
    o_ref[...] = jnp.dot(a_ref[...], b_ref[...], preferred_element_type=F32).astype(o_ref.dtype)


def _mm_gate_kernel(a_ref, b_ref, bias_ref, o_ref):
    acc = jnp.dot(a_ref[...], b_ref[...], preferred_element_type=F32)
    o_ref[...] = _sigmoid(acc + bias_ref[...]).astype(o_ref.dtype)


def _mm_res_kernel(a_ref, b_ref, x_ref, o_ref):
    o_ref[...] = x_ref[...] + jnp.dot(a_ref[...], b_ref[...], preferred_element_type=F32)


def _matmul(a, b, out_dtype, tm, tn, name, bias=None, residual=None):
    m, k = a.shape
    n = b.shape[1]
    tm = min(tm, m)
    in_specs = [pl.BlockSpec((tm, k), lambda i, j: (i, 0)), pl.BlockSpec((k, tn), lambda i, j: (0, j))]
    args = [a, b]
    kern = _mm_kernel
    if bias is not None:
        in_specs.append(pl.BlockSpec((1, tn), lambda i, j: (0, j)))
        args.append(bias.reshape(1, n))
        kern = _mm_gate_kernel
    if residual is not None:
        in_specs.append(pl.BlockSpec((tm, tn), lambda i, j: (i, j)))
        args.append(residual)
        kern = _mm_res_kernel
    return pl.pallas_call(
        kern,
        out_shape=jax.ShapeDtypeStruct((m, n), out_dtype),
        grid=(m // tm, n // tn),
        in_specs=in_specs,
        out_specs=pl.BlockSpec((tm, tn), lambda i, j: (i, j)),
        compiler_params=_params(2),
        name=name,
    )(*args)


def _mix_kernel(a1_ref, b1_ref, a2_ref, b2_ref, g1_ref, g2_ref, o_ref):
    y1 = jnp.dot(a1_ref[...], b1_ref[...], preferred_element_type=F32)
    y2 = jnp.dot(a2_ref[...], b2_ref[...], preferred_element_type=F32)
    o_ref[...] = (g1_ref[...].astype(F32) * y1 + g2_ref[...].astype(F32) * y2).astype(o_ref.dtype)


def _branch_mix(o_rw, o_dn, wb_rw, wb_dn, gates, tm=1024, tn=512):
    m, k = o_rw.shape
    n = wb_rw.shape[1]
    tm = min(tm, m)
    goff = n // tn
    return pl.pallas_call(
        _mix_kernel,
        out_shape=jax.ShapeDtypeStruct((m, n), BF16),
        grid=(m // tm, n // tn),
        in_specs=[pl.BlockSpec((tm, k), lambda i, j: (i, 0)), pl.BlockSpec((k, tn), lambda i, j: (0, j)),
                  pl.BlockSpec((tm, k), lambda i, j: (i, 0)), pl.BlockSpec((k, tn), lambda i, j: (0, j)),
                  pl.BlockSpec((tm, tn), lambda i, j: (i, j)),
                  pl.BlockSpec((tm, tn), lambda i, j: (i, j + goff))],
        out_specs=pl.BlockSpec((tm, tn), lambda i, j: (i, j)),
        compiler_params=_params(2),
        name="branch_mix",
    )(o_rw, wb_rw, o_dn, wb_dn, gates, gates)


def _tri_incl():
    r = lax.broadcasted_iota(jnp.int32, (CHUNK, CHUNK), 0)
    c = lax.broadcasted_iota(jnp.int32, (CHUNK, CHUNK), 1)
    return (r >= c).astype(BF16)


def _eye_bf16(n):
    r = lax.broadcasted_iota(jnp.int32, (n, n), 0)
    c = lax.broadcasted_iota(jnp.int32, (n, n), 1)
    return (r == c).astype(BF16)


def _cumsum_rows(x):
    tri = _tri_incl()
    hi, lo = _split2(x)
    return (jnp.dot(tri, hi, preferred_element_type=F32) + jnp.dot(tri, lo, preferred_element_type=F32))


def _gdn_kernel(p_ref, halo_ref, cw_ref, alog_ref, dtb_ref, nw_ref, o_ref, ext_ref, state_ref):
    s = pl.program_id(1)

    @pl.when(s == 0)
    def _():
        state_ref[...] = jnp.zeros_like(state_ref)

    ext_ref[0:HALO_ROWS, :] = jnp.where(s == 0, 0.0, halo_ref[:, 0:DN_QKV])
    ext_ref[HALO_ROWS:HALO_ROWS + CHUNK, :] = p_ref[:, 0:DN_QKV]

    def conv_silu(col):
        cs = slice(col, col + DN_HEAD)
        acc = ext_ref[HALO_ROWS:HALO_ROWS + CHUNK, cs] * cw_ref[DN_CONV - 1:DN_CONV, cs]
        for j in range(DN_CONV - 1):
            r0 = HALO_ROWS - (DN_CONV - 1) + j
            acc = acc + ext_ref[r0:r0 + CHUNK, cs] * cw_ref[j:j + 1, cs]
        return _silu(acc)

    def l2norm(t):
        return t * lax.rsqrt(jnp.sum(t * t, axis=-1, keepdims=True) + NORM_EPS)

    small = p_ref[:, DN_OFF_SMALL:DN_OFF_SMALL + LORA_PAD]
    beta_all = _sigmoid(small)
    sp_in = small + dtb_ref[...]
    softplus = jnp.maximum(sp_in, 0.0) + jnp.log1p(jnp.exp(-jnp.abs(sp_in)))
    g_all = -jnp.exp(alog_ref[...]) * softplus
    gcum = _cumsum_rows(g_all)
    eye = _eye_bf16(LANES)
    ghi, glo = _split2(gcum)
    gcum_t = (lax.dot_general(eye, ghi, (((1,), (1,)), ((), ())), preferred_element_type=F32)
              + lax.dot_general(eye, glo, (((1,), (1,)), ((), ())), preferred_element_type=F32))

    r = lax.broadcasted_iota(jnp.int32, (CHUNK, CHUNK), 0)
    c = lax.broadcasted_iota(jnp.int32, (CHUNK, CHUNK), 1)
    causal = r >= c
    strict = r > c
    eye_c = jnp.where(r == c, 1.0, 0.0)
    nw = nw_ref[...]
    heads = range(DN_HEADS)

    q = [l2norm(conv_silu(h * DN_HEAD)) * (DN_HEAD ** -0.5) for h in heads]
    k = [l2norm(conv_silu(DN_KEY + h * DN_HEAD)) for h in heads]
    v = [conv_silu(2 * DN_KEY + h * DN_HEAD) for h in heads]
    gcol = [gcum[:, DN_HEADS + h:DN_HEADS + h + 1] for h in heads]
    glast = [g[CHUNK - 1:CHUNK, :] for g in gcol]
    decay = [jnp.where(causal, jnp.exp(jnp.where(causal, gcol[h] - gcum_t[DN_HEADS + h:DN_HEADS + h + 1, :], 0.0)),
                       0.0) for h in heads]
    beta = [beta_all[:, h:h + 1] for h in heads]
    kb = [k[h] * beta[h] for h in heads]
    eg = [jnp.exp(g) for g in gcol]
    kk = [_dot_nt(jnp.concatenate([kb[h], q[h]], axis=0), k[h]) for h in heads]
    k_tail_t = [_dot_nt(eye, k[h] * jnp.exp(glast[h] - gcol[h])) for h in heads]
    attn = [kk[h][CHUNK:] * decay[h] for h in heads]
    m = [-jnp.where(strict, kk[h][:CHUNK] * decay[h], 0.0) for h in heads]
    inv = [eye_c + m[h] for h in heads]
    m = [_dot(m[h], m[h]) for h in heads]
    for _ in range(4):
        both = [_dot(jnp.concatenate([m[h], inv[h]], axis=0), m[h]) for h in heads]
        inv = [inv[h] + both[h][CHUNK:] for h in heads]
        m = [both[h][:CHUNK] for h in heads]
    inv = [inv[h] + _dot(inv[h], m[h]) for h in heads]
    sol = [_dot(inv[h], jnp.concatenate([v[h] * beta[h], kb[h] * eg[h]], axis=1)) for h in heads]
    st = [state_ref[h] for h in heads]
    wq = [_dot(jnp.concatenate([sol[h][:, DN_HEAD:], q[h] * eg[h]], axis=0), st[h]) for h in heads]
    v_new = [sol[h][:, :DN_HEAD] - wq[h][:CHUNK] for h in heads]
    o = [wq[h][CHUNK:] + _dot(attn[h], v_new[h]) for h in heads]
    for h in heads:
        state_ref[h] = st[h] * jnp.exp(glast[h]) + _dot(k_tail_t[h], v_new[h])
    for h in heads:
        on = o[h] * lax.rsqrt(jnp.mean(o[h] * o[h], axis=-1, keepdims=True) + NORM_EPS) * nw
        z = p_ref[:, DN_OFF_Z + h * DN_HEAD:DN_OFF_Z + (h + 1) * DN_HEAD]
        o_ref[:, h * DN_HEAD:(h + 1) * DN_HEAD] = (on * _silu(z)).astype(o_ref.dtype)


def _gated_deltanet(p_dn, batch, seq, conv_w, a_log, dt_bias, norm_w):
    t = batch * seq
    ns = seq // CHUNK
    hb = CHUNK // HALO_ROWS
    alog = jnp.zeros((1, LORA_PAD), F32).at[0, DN_HEADS:2 * DN_HEADS].set(a_log)
    dtb = jnp.zeros((1, LORA_PAD), F32).at[0, DN_HEADS:2 * DN_HEADS].set(dt_bias)
    return pl.pallas_call(
        _gdn_kernel,
        out_shape=jax.ShapeDtypeStruct((t, DN_VAL), BF16),
        grid=(batch, ns),
        in_specs=[pl.BlockSpec((CHUNK, DN_PCOLS), lambda b, s: (b * ns + s, 0)),
                  pl.BlockSpec((HALO_ROWS, DN_PCOLS), lambda b, s: (jnp.maximum((b * ns + s) * hb - 1, 0), 0)),
                  pl.BlockSpec((DN_CONV, DN_QKV), lambda b, s: (0, 0)),
                  pl.BlockSpec((1, LORA_PAD), lambda b, s: (0, 0)),
                  pl.BlockSpec((1, LORA_PAD), lambda b, s: (0, 0)),
                  pl.BlockSpec((1, DN_HEAD), lambda b, s: (0, 0))],
        out_specs=pl.BlockSpec((CHUNK, DN_VAL), lambda b, s: (b * ns + s, 0)),
        scratch_shapes=[pltpu.VMEM((HALO_ROWS + CHUNK, DN_QKV), F32),
                        pltpu.VMEM((DN_HEADS, DN_HEAD, DN_HEAD), F32)],
        compiler_params=_params(2),
        name="gated_deltanet",
    )(p_dn, p_dn, conv_w, alog, dtb, norm_w.reshape(1, DN_HEAD))


def _rwkv_kernel(p_ref, halo_ref, mu_ref, w0_ref, a0_ref, kk_ref, ka_ref, rk_ref, lnw_ref, lnb_ref,
                 w2_ref, a2_ref, g2_ref, o_ref, state_ref):
    s = pl.program_id(1)

    @pl.when(s == 0)
    def _():
        state_ref[...] = jnp.zeros_like(state_ref)

    row1 = lax.broadcasted_iota(jnp.int32, (CHUNK, 1), 0)

    def shifted(lo, hi):
        cur = p_ref[:, lo:hi]
        last = jnp.where(s == 0, 0.0, halo_ref[HALO_ROWS - 1:HALO_ROWS, lo:hi])
        prev = jnp.where(row1 == 0, last, pltpu.roll(cur, 1, 0))
        return cur + (prev - cur) * mu_ref[:, lo:hi]

    wd = shifted(RW_OFF_WD, RW_OFF_WD + LORA_PAD)
    ad = shifted(RW_OFF_AD, RW_OFF_AD + LORA_PAD)
    gd = shifted(RW_OFF_GD, RW_OFF_GD + RW_GATE_LORA)
    log_w = -RW_DECAY_SCALE * _sigmoid(w0_ref[...] + _dot(jnp.tanh(wd), w2_ref[...]))
    a_all = _sigmoid(a0_ref[...] + _dot(ad, a2_ref[...]))
    g_all = _dot(_sigmoid(gd), g2_ref[...])

    lane = lax.broadcasted_iota(jnp.int32, (CHUNK, LANES), 1)
    row = lax.broadcasted_iota(jnp.int32, (CHUNK, LANES), 0)
    tcol = lane % RW_HEAD
    strict_ss = row > tcol
    incl_ss = row >= tcol
    eye_ss = jnp.where(row == tcol, 1.0, 0.0)
    r2 = lax.broadcasted_iota(jnp.int32, (LANES, LANES), 0)
    c2 = lax.broadcasted_iota(jnp.int32, (LANES, LANES), 1)
    bd_mask = (r2 // RW_HEAD) == (c2 // RW_HEAD)
    ones_bd = bd_mask.astype(BF16)
    head0 = lane < RW_HEAD
    eye = _eye_bf16(LANES)

    def seg_sum(x):
        s0 = jnp.sum(jnp.where(head0, x, 0.0), axis=-1, keepdims=True)
        s1 = jnp.sum(jnp.where(head0, 0.0, x), axis=-1, keepdims=True)
        return jnp.where(head0, s0, s1)

    def block_diag(x):
        return jnp.where(bd_mask, jnp.concatenate([x, x], axis=0), 0.0)

    def per_head_rows(x):
        return jnp.concatenate([jnp.where(head0, x, 0.0), jnp.where(head0, 0.0, x)], axis=0)

    def run_pairs(pairs):
        sl = {p: slice(p * LANES, (p + 1) * LANES) for p in pairs}
        r = {p: shifted(RW_OFF_R + p * LANES, RW_OFF_R + (p + 1) * LANES) for p in pairs}
        k = {p: shifted(RW_OFF_K + p * LANES, RW_OFF_K + (p + 1) * LANES) for p in pairs}
        v = {p: shifted(RW_OFF_V + p * LANES, RW_OFF_V + (p + 1) * LANES) for p in pairs}
        lw = {p: log_w[:, sl[p]] for p in pairs}
        a = {p: a_all[:, sl[p]] for p in pairs}
        kkr = {p: k[p] * kk_ref[:, sl[p]] for p in pairs}
        kss = {p: seg_sum(kkr[p] * kkr[p]) for p in pairs}
        gc = {p: _cumsum_rows(lw[p]) for p in pairs}
        kk = {p: kkr[p] * lax.rsqrt(kss[p] + NORM_EPS) for p in pairs}
        kh = {p: k[p] * (1.0 + (a[p] - 1.0) * ka_ref[:, sl[p]]) for p in pairs}
        b = {p: kk[p] * a[p] for p in pairs}
        bonus_s = {p: seg_sum(r[p] * kh[p] * rk_ref[:, sl[p]]) for p in pairs}
        glast = {p: gc[p][CHUNK - 1:CHUNK, :] for p in pairs}
        e_neg = {p: jnp.exp(-gc[p]) for p in pairs}
        e_tail = {p: jnp.exp(glast[p] - gc[p]) for p in pairs}
        x1 = {p: jnp.concatenate([kk[p] * jnp.exp(gc[p] - lw[p]), r[p] * jnp.exp(gc[p])], axis=0) for p in pairs}
        x3 = {p: jnp.concatenate([kh[p] * e_tail[p], b[p] * e_tail[p]], axis=0) for p in pairs}
        st = {p: state_ref[p] for p in pairs}
        akb = {p: _dot_nt(x1[p], jnp.concatenate([per_head_rows(kh[p] * e_neg[p]),
                                                  per_head_rows(b[p] * e_neg[p]), st[p]], axis=0)) for p in pairs}
        ak = {p: akb[p][:, 0:LANES] for p in pairs}
        ab = {p: akb[p][:, LANES:2 * LANES] for p in pairs}
        xh = {p: akb[p][:, 2 * LANES:3 * LANES] for p in pairs}
        bd_v = {p: block_diag(v[p]) for p in pairs}
        rhs = {p: xh[p][:CHUNK] + _dot(jnp.where(strict_ss, ak[p][:CHUNK], 0.0), bd_v[p]) for p in pairs}
        m = {p: -jnp.where(strict_ss, ab[p][:CHUNK], 0.0) for p in pairs}
        inv = {p: eye_ss + m[p] for p in pairs}
        m = {p: _dot(m[p], block_diag(m[p])) for p in pairs}
        for _ in range(4):
            both = {p: _dot(jnp.concatenate([m[p], inv[p]], axis=0), block_diag(m[p])) for p in pairs}
            inv = {p: inv[p] + both[p][CHUNK:] for p in pairs}
            m = {p: both[p][:CHUNK] for p in pairs}
        inv = {p: inv[p] + _dot(inv[p], block_diag(m[p])) for p in pairs}
        u = {p: _dot(inv[p], block_diag(rhs[p])) for p in pairs}
        o = {p: xh[p][CHUNK:] + _dot(jnp.concatenate([jnp.where(incl_ss, ak[p][CHUNK:], 0.0),
                                                      jnp.where(incl_ss, ab[p][CHUNK:], 0.0)], axis=1),
                                     jnp.concatenate([bd_v[p], -block_diag(u[p])], axis=0)) for p in pairs}
        vu_t = {p: jnp.concatenate([v[p], -u[p]], axis=0).T for p in pairs}
        for p in pairs:
            state_ref[p] = st[p] * jnp.exp(glast[p]) + jnp.where(bd_mask, _dot(vu_t[p], x3[p]), 0.0)
        oc = {p: o[p] - seg_sum(o[p]) * (1.0 / RW_HEAD) for p in pairs}
        var = {p: seg_sum(oc[p] * oc[p]) * (1.0 / RW_HEAD) for p in pairs}
        for p in pairs:
            on = oc[p] * lax.rsqrt(var[p] + RW_GN_EPS) * lnw_ref[:, sl[p]] + lnb_ref[:, sl[p]]
            o_ref[:, sl[p]] = ((on + bonus_s[p] * v[p]) * g_all[:, sl[p]]).astype(o_ref.dtype)

    n_pairs = RW_HEADS // 2
    for grp in range(n_pairs // RW_PAIR_GROUP):
        run_pairs(range(grp * RW_PAIR_GROUP, (grp + 1) * RW_PAIR_GROUP))


def _rwkv7(p_rw, batch, seq, mu, w0, w2, a0, a2, g2, k_k, k_a, r_k, ln_w, ln_b):
    t = batch * seq
    ns = seq // CHUNK
    hb = CHUNK // HALO_ROWS
    vec = lambda x: x.reshape(1, RW_WIDTH)
    cvec = pl.BlockSpec((1, RW_WIDTH), lambda b, s: (0, 0))
    return pl.pallas_call(
        _rwkv_kernel,
        out_shape=jax.ShapeDtypeStruct((t, RW_WIDTH), BF16),
        grid=(batch, ns),
        in_specs=[pl.BlockSpec((CHUNK, RW_PCOLS), lambda b, s: (b * ns + s, 0)),
                  pl.BlockSpec((HALO_ROWS, RW_PCOLS), lambda b, s: (jnp.maximum((b * ns + s) * hb - 1, 0), 0)),
                  pl.BlockSpec((1, RW_PCOLS), lambda b, s: (0, 0)),
                  cvec, cvec, cvec, cvec, cvec, cvec, cvec,
                  pl.BlockSpec((LORA_PAD, RW_WIDTH), lambda b, s: (0, 0)),
                  pl.BlockSpec((LORA_PAD, RW_WIDTH), lambda b, s: (0, 0)),
                  pl.BlockSpec((RW_GATE_LORA, RW_WIDTH), lambda b, s: (0, 0))],
        out_specs=pl.BlockSpec((CHUNK, RW_WIDTH), lambda b, s: (b * ns + s, 0)),
        scratch_shapes=[pltpu.VMEM((RW_HEADS // 2, LANES, LANES), F32)],
        compiler_params=_params(2),
        name="rwkv7",
    )(p_rw, p_rw, mu.reshape(1, RW_PCOLS), vec(w0), vec(a0), vec(k_k), vec(k_a), vec(r_k), vec(ln_w),
      vec(ln_b), w2, a2, g2)


PACK_ROWS = D_MODEL // (2 * LANES)
PACK_PITCH = PACK_ROWS + 2


def _pack_pair(a, b):
    ua = lax.bitcast_convert_type(a.astype(BF16).astype(F32), jnp.uint32)
    ub = lax.bitcast_convert_type(b.astype(BF16).astype(F32), jnp.uint32)
    return ua | lax.shift_right_logical(ub, jnp.uint32(16))


def _unpack_pair(u):
    a = lax.bitcast_convert_type(u & jnp.uint32(0xFFFF0000), F32)
    b = lax.bitcast_convert_type(lax.shift_left(u, jnp.uint32(16)), F32)
    return a, b


def _store_packed(ref, rows, x):
    half = D_MODEL // 2
    for j in range(PACK_ROWS):
        ref[pl.ds(j, rows, stride=PACK_PITCH), :] = _pack_pair(x[:, j * LANES:(j + 1) * LANES],
                                                               x[:, half + j * LANES:half + (j + 1) * LANES])
    for j in range(PACK_ROWS, PACK_PITCH):
        ref[pl.ds(j, rows, stride=PACK_PITCH), :] = jnp.zeros((rows, LANES), jnp.uint32)


def _router_kernel(x_ref, g_ref, wr_ref, br_ref, xpk_ref, eid_ref, wgt_ref):
    x = x_ref[...]
    xn = x * lax.rsqrt(jnp.mean(x * x, axis=-1, keepdims=True) + NORM_EPS) * g_ref[...]
    _store_packed(xpk_ref, x.shape[0], xn)
    xh, xl = _split2(xn)
    wh, wl = _split2(wr_ref[...])
    logits = (jnp.dot(xh, wh, preferred_element_type=F32) + jnp.dot(xh, wl, preferred_element_type=F32)
              + jnp.dot(xl, wh, preferred_element_type=F32)) + br_ref[...]
    tm = logits.shape[0]
    lane = lax.broadcasted_iota(jnp.int32, (tm, LANES), 1)
    neg = -1e30
    big = 4 * LANES

    def first_argmax(vals):
        mx = jnp.max(vals, axis=-1, keepdims=True)
        idx = jnp.min(jnp.where(vals == mx, lane, big), axis=-1, keepdims=True)
        return mx, idx

    gl = jnp.where(lane < N_GROUPS, logits, neg)
    gmax, gi = first_argmax(gl)
    gp = 1.0 / jnp.sum(jnp.where(lane < N_GROUPS, jnp.exp(gl - gmax), 0.0), axis=-1, keepdims=True)
    e_lo = N_GROUPS + gi * EXPERTS_PER_GROUP
    el = jnp.where((lane >= e_lo) & (lane < e_lo + EXPERTS_PER_GROUP), logits, neg)
    m1, i1 = first_argmax(el)
    m2, i2 = first_argmax(jnp.where(lane == i1, neg, el))
    e2 = jnp.exp(m2 - m1)
    w1 = gp / (1.0 + e2)
    w2 = gp * e2 / (1.0 + e2)
    eid_ref[...] = jnp.where(lane == 0, i1 - N_GROUPS, jnp.where(lane == 1, i2 - N_GROUPS, 0))
    wgt_ref[...] = jnp.where(lane == 0, w1, jnp.where(lane == 1, w2, 0.0))


def _router(x_mid, norm_g, wr, br, tm=256):
    t, d = x_mid.shape
    tm = min(tm, t)
    return pl.pallas_call(
        _router_kernel,
        out_shape=(jax.ShapeDtypeStruct((t * PACK_PITCH, LANES), jnp.uint32),
                   jax.ShapeDtypeStruct((t, LANES), jnp.int32),
                   jax.ShapeDtypeStruct((t, LANES), F32)),
        grid=(t // tm,),
        in_specs=[pl.BlockSpec((tm, d), lambda i: (i, 0)), pl.BlockSpec((1, d), lambda i: (0, 0)),
                  pl.BlockSpec((d, LANES), lambda i: (0, 0)), pl.BlockSpec((1, LANES), lambda i: (0, 0))],
        out_specs=(pl.BlockSpec((tm * PACK_PITCH, LANES), lambda i: (i, 0)),
                   pl.BlockSpec((tm, LANES), lambda i: (i, 0)), pl.BlockSpec((tm, LANES), lambda i: (i, 0))),
        compiler_params=_params(1),
        name="router",
    )(x_mid, norm_g.reshape(1, d), wr, br)


DMA_GROUP = 8


def _expert_changed(be_ref, i):
    return jnp.logical_or(i == 0, be_ref[i] != be_ref[jnp.maximum(i - 1, 0)])


def _moe_up_kernel(be_ref, nv_ref, row_ref, x_hbm, w1_ref, w3_ref, h_ref, xbuf, xb, sem):
    del be_ref
    i = pl.program_id(0)
    j = pl.program_id(1)
    nb = pl.num_programs(0)
    slot = i % 2
    half = D_MODEL // 2
    rows = h_ref.shape[0]

    def gather(blk, sl, wait):
        n = nv_ref[blk]
        ng = lax.shift_right_logical(n, 3)

        def row_copy(r):
            src = row_ref[blk * rows + r]
            return pltpu.make_async_copy(x_hbm.at[pl.ds(src, PACK_ROWS)],
                                         xbuf.at[sl, pl.ds(r * PACK_PITCH, PACK_ROWS)], sem.at[sl])

        def group(g, carry):
            if wait:
                n_rows = DMA_GROUP * PACK_ROWS
                pltpu.make_async_copy(x_hbm.at[pl.ds(0, n_rows)], xbuf.at[sl, pl.ds(0, n_rows)],
                                      sem.at[sl]).wait()
            else:
                for q in range(DMA_GROUP):
                    row_copy(g * DMA_GROUP + q).start()
            return carry

        def single(r, carry):
            if wait:
                row_copy(r).wait()
            else:
                row_copy(r).start()
            return carry

        lax.fori_loop(0, ng, group, 0)
        lax.fori_loop(ng * DMA_GROUP, n, single, 0)

    @pl.when(jnp.logical_and(i == 0, j == 0))
    def _():
        xbuf[...] = jnp.zeros_like(xbuf)
        gather(0, 0, False)

    @pl.when(jnp.logical_and(j == 0, i + 1 < nb))
    def _():
        gather(jnp.minimum(i + 1, nb - 1), 1 - slot, False)

    @pl.when(j == 0)
    def _():
        gather(i, slot, True)

    @pl.when(jnp.logical_and(j == 0, nv_ref[i] > 0))
    def _():
        xs = xbuf.at[slot]
        for c in range(PACK_ROWS):
            hi, lo = _unpack_pair(xs[pl.ds(c, rows, stride=PACK_PITCH), :])
            xb[:, c * LANES:(c + 1) * LANES] = hi.astype(BF16)
            xb[:, half + c * LANES:half + (c + 1) * LANES] = lo.astype(BF16)

    def up_rows(m):
        x = xb[0:m, :]
        h1 = jnp.dot(x, w1_ref[0].astype(BF16), preferred_element_type=F32)
        h3 = jnp.dot(x, w3_ref[0].astype(BF16), preferred_element_type=F32)
        h_ref[0:m, :] = (_silu(h1) * h3).astype(h_ref.dtype)

    @pl.when(nv_ref[i] > rows // 2)
    def _():
        up_rows(rows)

    @pl.when(jnp.logical_and(nv_ref[i] > 0, nv_ref[i] <= rows // 2))
    def _():
        up_rows(rows // 2)
        h_ref[rows // 2:, :] = jnp.zeros((rows - rows // 2, h_ref.shape[1]), h_ref.dtype)

    @pl.when(nv_ref[i] == 0)
    def _():
        h_ref[...] = jnp.zeros_like(h_ref)


def _moe_down_kernel(be_ref, nv_ref, dst_ref, h_ref, w2_ref, y_hbm, ybuf, wb2, sem):
    i = pl.program_id(0)
    nb = pl.num_programs(0)
    slot = i % 2
    rows = h_ref.shape[0]

    def row_copy(blk, sl, r):
        d = dst_ref[blk * rows + r]
        return pltpu.make_async_copy(ybuf.at[sl, pl.ds(r * PACK_PITCH, PACK_PITCH)],
                                     y_hbm.at[pl.ds(d, PACK_PITCH)], sem.at[sl])

    def scatter(blk, sl, wait):
        n = nv_ref[blk]
        ng = lax.shift_right_logical(n, 3)

        def group(g, carry):
            if wait:
                n_rows = DMA_GROUP * PACK_PITCH
                r0 = pl.multiple_of(g * n_rows, 8)
                pltpu.make_async_copy(ybuf.at[sl, pl.ds(r0, n_rows)], y_hbm.at[pl.ds(0, n_rows)],
                                      sem.at[sl]).wait()
            else:
                for j in range(DMA_GROUP):
                    row_copy(blk, sl, g * DMA_GROUP + j).start()
            return carry

        def single(r, carry):
            cp = row_copy(blk, sl, r)
            if wait:
                cp.wait()
            else:
                cp.start()
            return carry

        lax.fori_loop(0, ng, group, 0)
        lax.fori_loop(ng * DMA_GROUP, n, single, 0)

    prev2 = jnp.maximum(i - 2, 0)

    @pl.when(i >= 2)
    def _():
        scatter(prev2, slot, True)

    @pl.when(nv_ref[i] > 0)
    def _():
        @pl.when(_expert_changed(be_ref, i))
        def _():
            wb2[...] = w2_ref[0].astype(BF16)

        _store_packed(ybuf.at[slot], rows, jnp.dot(h_ref[...], wb2[...], preferred_element_type=F32))
        scatter(i, slot, False)

    prev1 = jnp.maximum(i - 1, 0)

    @pl.when(jnp.logical_and(i == nb - 1, i >= 1))
    def _():
        scatter(prev1, 1 - slot, True)

    @pl.when(i == nb - 1)
    def _():
        scatter(i, slot, True)


def _moe_experts(xpk, t, eid, w1, w3, w2):
    d = D_MODEL
    a = t * TOP_K
    e_flat = eid.reshape(a)
    tok_flat = jnp.repeat(jnp.arange(t, dtype=jnp.int32), TOP_K)
    dst_flat = (jnp.arange(a, dtype=jnp.int32) % TOP_K) * t + tok_flat
    onehot = (e_flat[:, None] == jnp.arange(N_EXPERTS, dtype=jnp.int32)[None, :]).astype(jnp.int32)
    csum = jnp.cumsum(onehot, axis=0)
    rank = jnp.sum(csum * onehot, axis=1) - 1
    counts = csum[-1]
    pcounts = (counts + MOE_PAD - 1) // MOE_PAD * MOE_PAD
    pend = jnp.cumsum(pcounts)
    poffs = pend - pcounts
    dest = poffs[e_flat] + rank
    p_rows = a + N_EXPERTS * MOE_PAD
    slot_dst = jnp.zeros((p_rows,), jnp.int32).at[dest].set(dst_flat)
    slot_src = (slot_dst % t) * PACK_PITCH
    slot_dst = slot_dst * PACK_PITCH

    def blocks(rows):
        starts = jnp.arange(p_rows // rows, dtype=jnp.int32) * rows
        block_e = jnp.minimum(jnp.searchsorted(pend, starts, side='right'), N_EXPERTS - 1).astype(jnp.int32)
        block_v = jnp.clip(poffs[block_e] + counts[block_e] - starts, 0, rows).astype(jnp.int32)
        last_e = block_e[jnp.maximum(jnp.sum(starts < pend[-1]) - 1, 0)]
        return jnp.where(starts < pend[-1], block_e, last_e), block_v

    up_e, up_v = blocks(UP_ROWS)

    def w_index(i, j, be, bv, tk):
        return be[i], 0, jnp.where(bv[i] > 0, j, UP_TILES - 1)

    h = pl.pallas_call(
        _moe_up_kernel,
        out_shape=jax.ShapeDtypeStruct((p_rows, D_EXPERT), BF16),
        grid_spec=pltpu.PrefetchScalarGridSpec(
            num_scalar_prefetch=3,
            grid=(p_rows // UP_ROWS, UP_TILES),
            in_specs=[pl.BlockSpec(memory_space=pl.ANY), pl.BlockSpec((1, d, UP_TILE), w_index),
                      pl.BlockSpec((1, d, UP_TILE), w_index)],
            out_specs=pl.BlockSpec((UP_ROWS, UP_TILE), lambda i, j, be, bv, tk: (i, j)),
            scratch_shapes=[pltpu.VMEM((2, UP_ROWS * PACK_PITCH, LANES), jnp.uint32),
                            pltpu.VMEM((UP_ROWS, d), BF16), pltpu.SemaphoreType.DMA((2,))]),
        compiler_params=_params(2),
        name="moe_up",
    )(up_e, up_v, slot_src, xpk, w1, w3)

    down_e, down_v = blocks(DOWN_ROWS)
    y = pl.pallas_call(
        _moe_down_kernel,
        out_shape=jax.ShapeDtypeStruct((TOP_K * t * PACK_PITCH, LANES), jnp.uint32),
        grid_spec=pltpu.PrefetchScalarGridSpec(
            num_scalar_prefetch=3,
            grid=(p_rows // DOWN_ROWS,),
            in_specs=[pl.BlockSpec((DOWN_ROWS, D_EXPERT), lambda i, be, bv, ds: (i, 0)),
                      pl.BlockSpec((1, D_EXPERT, d), lambda i, be, bv, ds: (be[i], 0, 0))],
            out_specs=pl.BlockSpec(memory_space=pl.ANY),
            scratch_shapes=[pltpu.VMEM((2, DOWN_ROWS * PACK_PITCH, LANES), jnp.uint32),
                            pltpu.VMEM((D_EXPERT, d), BF16), pltpu.SemaphoreType.DMA((2,))]),
        compiler_params=_params(1),
        name="moe_down",
    )(down_e, down_v, slot_dst, h, w2)
    return y


def _final_kernel(x_ref, y0_ref, y1_ref, wgt_ref, g_ref, o_ref):
    tm = x_ref.shape[0]
    half = D_MODEL // 2
    w0 = wgt_ref[:, 0:1]
    w1 = wgt_ref[:, 1:2]
    ss = jnp.zeros((tm, 1), F32)
    for j in range(PACK_ROWS):
        a0, b0 = _unpack_pair(y0_ref[pl.ds(j, tm, stride=PACK_PITCH), :])
        a1, b1 = _unpack_pair(y1_ref[pl.ds(j, tm, stride=PACK_PITCH), :])
        ca = slice(j * LANES, (j + 1) * LANES)
        cb = slice(half + j * LANES, half + (j + 1) * LANES)
        za = x_ref[:, ca] + w0 * a0 + w1 * a1
        zb = x_ref[:, cb] + w0 * b0 + w1 * b1
        o_ref[:, ca] = za
        o_ref[:, cb] = zb
        ss = ss + jnp.sum(za * za, axis=-1, keepdims=True) + jnp.sum(zb * zb, axis=-1, keepdims=True)
    o_ref[...] = o_ref[...] * lax.rsqrt(ss * (1.0 / D_MODEL) + NORM_EPS) * g_ref[...]


def _final(x_mid, y, wgt, g, tm=256):
    t, d = x_mid.shape
    tm = min(tm, t)
    nblk = t // tm
    return pl.pallas_call(
        _final_kernel,
        out_shape=jax.ShapeDtypeStruct((t, d), F32),
        grid=(nblk,),
        in_specs=[pl.BlockSpec((tm, d), lambda i: (i, 0)),
                  pl.BlockSpec((tm * PACK_PITCH, LANES), lambda i: (i, 0)),
                  pl.BlockSpec((tm * PACK_PITCH, LANES), lambda i: (i + nblk, 0)),
                  pl.BlockSpec((tm, LANES), lambda i: (i, 0)), pl.BlockSpec((1, d), lambda i: (0, 0))],
        out_specs=pl.BlockSpec((tm, d), lambda i: (i, 0)),
        compiler_params=_params(1),
        name="final_combine",
    )(x_mid, y, y, wgt, g.reshape(1, d))


def _pad_cols(w, n):
    return jnp.pad(w, ((0, 0), (0, n - w.shape[1])))


def _pad_rows(w, n):
    return jnp.pad(w, ((0, n - w.shape[0]), (0, 0)))


def _layer(x2, batch, seq, norm1_g, w_in, rw_mu, rw_w0, rw_w2, rw_a0, rw_a2, rw_g2, rw_k_k, rw_k_a, rw_r_k,
           rw_ln_w, rw_ln_b, dn_conv_w, dn_a_log, dn_dt_bias, dn_norm_w, gate_b, w_branch, w_out, norm2_g,
           moe_gr_w, moe_gr_b, moe_er_w, moe_er_b, moe_w1, moe_w3, moe_w2):
    sizes = (RW_WIDTH, RW_DECAY_LORA, RW_WIDTH, RW_WIDTH, RW_A_LORA, RW_GATE_LORA,
             DN_QKV, 2 * DN_HEADS, DN_VAL, 2 * D_MODEL)
    offs = [0]
    for sz in sizes:
        offs.append(offs[-1] + sz)
    cols = lambda w, i: w[..., offs[i]:offs[i + 1]]
    w_bf = w_in.astype(BF16)
    w_rw = jnp.concatenate([cols(w_bf, 0), _pad_cols(cols(w_bf, 1), LORA_PAD), cols(w_bf, 2), cols(w_bf, 3),
                            _pad_cols(cols(w_bf, 4), LORA_PAD), cols(w_bf, 5)], axis=1)
    w_dn = _pad_cols(jnp.concatenate([cols(w_bf, 6), _pad_cols(cols(w_bf, 7), LORA_PAD), cols(w_bf, 8)], axis=1),
                     DN_PCOLS)
    w_gate = cols(w_bf, 9)
    mu2 = rw_mu.reshape(1, -1)
    mu = jnp.concatenate([cols(mu2, 0), _pad_cols(cols(mu2, 1), LORA_PAD), cols(mu2, 2), cols(mu2, 3),
                          _pad_cols(cols(mu2, 4), LORA_PAD), cols(mu2, 5)], axis=1)

    h = _rmsnorm(x2, norm1_g, BF16)
    p_rw = _matmul(h, w_rw, F32, 1024, 512, "proj_rwkv")
    p_dn = _matmul(h, w_dn, F32, 1024, 3 * MXU_WIDTH, "proj_gdn")
    gates = _matmul(h, w_gate, BF16, 1024, 1024, "proj_gate", bias=gate_b)

    o_rw = _rwkv7(p_rw, batch, seq, mu, rw_w0, _pad_rows(rw_w2, LORA_PAD).astype(BF16), rw_a0,
                  _pad_rows(rw_a2, LORA_PAD).astype(BF16), rw_g2.astype(BF16), rw_k_k, rw_k_a,
                  rw_r_k.reshape(-1), rw_ln_w, rw_ln_b)
    o_dn = _gated_deltanet(p_dn, batch, seq, dn_conv_w, dn_a_log, dn_dt_bias, dn_norm_w)

    mix = _branch_mix(o_rw, o_dn, w_branch[:RW_WIDTH].astype(BF16), w_branch[RW_WIDTH:].astype(BF16), gates)
    x_mid = _matmul(mix, w_out.astype(BF16), F32, 1024, 512, "proj_out", residual=x2)

    wr = _pad_cols(jnp.concatenate([moe_gr_w, moe_er_w], axis=1), LANES)
    br = _pad_cols(jnp.concatenate([moe_gr_b, moe_er_b]).reshape(1, -1), LANES)
    xpk, eid, wgt = _router(x_mid, norm2_g, wr, br)
    y = _moe_experts(xpk, x_mid.shape[0], eid[:, :TOP_K], moe_w1, moe_w3, moe_w2)
    return x_mid, y, wgt


def kernel(x, norm1_g, w_in, rw_mu, rw_w0, rw_w2, rw_a0, rw_a2, rw_g2, rw_k_k, rw_k_a, rw_r_k, rw_ln_w, rw_ln_b, dn_conv_w, dn_a_log, dn_dt_bias, dn_norm_w, gate_b, w_branch, w_out, norm2_g, moe_gr_w, moe_gr_b, moe_er_w, moe_er_b, moe_w1, moe_w3, moe_w2, final_g):
    batch, seq, d = x.shape
    assert norm1_g.shape[0] == 1, "single-layer block"
    x_mid, y, wgt = _layer(
        x.reshape(batch * seq, d), batch, seq, norm1_g[0], w_in[0], rw_mu[0], rw_w0[0], rw_w2[0], rw_a0[0],
        rw_a2[0], rw_g2[0], rw_k_k[0], rw_k_a[0], rw_r_k[0], rw_ln_w[0], rw_ln_b[0], dn_conv_w[0], dn_a_log[0],
        dn_dt_bias[0], dn_norm_w[0], gate_b[0], w_branch[0], w_out[0], norm2_g[0], moe_gr_w[0], moe_gr_b[0],
        moe_er_w[0], moe_er_b[0], moe_w1[0], moe_w3[0], moe_w2[0])
    return _final(x_mid, y, wgt, final_g).reshape(batch, seq, d)
```

```python
import functools

import jax
import jax.numpy as jnp
from jax import lax
from jax.experimental import pallas as pl
from jax.experimental.pallas import tpu as pltpu

F32 = jnp.float32
BF16 = jnp.bfloat16

D_MODEL = 4096
RW_HEADS = 32
RW_HEAD = 64
RW_WIDTH = RW_HEADS * RW_HEAD
RW_DECAY_LORA = 96
RW_A_LORA = 96
RW_GATE_LORA = 256
RW_GN_EPS = 64e-5
RW_DECAY_SCALE = 0.606531
DN_HEADS = 16
DN_HEAD = 128
DN_KEY = DN_HEADS * DN_HEAD
DN_VAL = DN_HEADS * DN_HEAD
DN_QKV = 2 * DN_KEY + DN_VAL
DN_CONV = 4
N_GROUPS = 8
EXPERTS_PER_GROUP = 8
N_EXPERTS = N_GROUPS * EXPERTS_PER_GROUP
TOP_K = 2
D_EXPERT = 768
NORM_EPS = 1e-6

LANES = 128
HALO_ROWS = 8
CHUNK = 64
RW_PAIR_GROUP = 16
LORA_PAD = 128
RW_PCOLS = 3 * RW_WIDTH + 2 * LORA_PAD + RW_GATE_LORA
MXU_WIDTH = 256
DN_PCOLS = -(-(DN_QKV + LORA_PAD + DN_VAL) // (3 * MXU_WIDTH)) * (3 * MXU_WIDTH)
RW_OFF_R = 0
RW_OFF_WD = RW_WIDTH
RW_OFF_K = RW_OFF_WD + LORA_PAD
RW_OFF_V = RW_OFF_K + RW_WIDTH
RW_OFF_AD = RW_OFF_V + RW_WIDTH
RW_OFF_GD = RW_OFF_AD + LORA_PAD
DN_OFF_SMALL = DN_QKV
DN_OFF_Z = DN_QKV + LORA_PAD
MOE_PAD = 512
UP_ROWS = 512
DOWN_ROWS = 256
UP_TILES = 3
UP_TILE = D_EXPERT // UP_TILES
VMEM_LIMIT = 56 * 1024 * 1024


def _params(n_axes):
    return pltpu.CompilerParams(dimension_semantics=("arbitrary",) * n_axes,
                                vmem_limit_bytes=VMEM_LIMIT)


def _dot(a, b):
    return jnp.dot(a.astype(BF16), b.astype(BF16), preferred_element_type=F32)


def _dot_nt(a, b):
    return lax.dot_general(a.astype(BF16), b.astype(BF16), (((1,), (1,)), ((), ())),
                           preferred_element_type=F32)


def _split2(x):
    hi = x.astype(BF16)
    lo = (x - hi.astype(F32)).astype(BF16)
    return hi, lo


def _sigmoid(x):
    return 1.0 / (1.0 + jnp.exp(-x))


def _silu(x):
    return x * _sigmoid(x)


def _rmsnorm_kernel(x_ref, g_ref, o_ref):
    x = x_ref[...]
    ms = jnp.mean(x * x, axis=-1, keepdims=True)
    o_ref[...] = (x * lax.rsqrt(ms + NORM_EPS) * g_ref[...]).astype(o_ref.dtype)


def _rmsnorm(x, g, out_dtype, tm=256):
    t, d = x.shape
    tm = min(tm, t)
    return pl.pallas_call(
        _rmsnorm_kernel,
        out_shape=jax.ShapeDtypeStruct((t, d), out_dtype),
        grid=(t // tm,),
        in_specs=[pl.BlockSpec((tm, d), lambda i: (i, 0)), pl.BlockSpec((1, d), lambda i: (0, 0))],
        out_specs=pl.BlockSpec((tm, d), lambda i: (i, 0)),
        compiler_params=_params(1),
        name="rmsnorm",
    )(x, g.reshape(1, d))


def _mm_kernel(a_ref, b_ref, o_ref):
    o_ref[...] = jnp.dot(a_ref[...], b_ref[...], preferred_element_type=F32).astype(o_ref.dtype)


def _mm_gate_kernel(a_ref, b_ref, bias_ref, o_ref):
    acc = jnp.dot(a_ref[...], b_ref[...], preferred_element_type=F32)
    o_ref[...] = _sigmoid(acc + bias_ref[...]).astype(o_ref.dtype)


def _mm_res_kernel(a_ref, b_ref, x_ref, o_ref):
    o_ref[...] = x_ref[...] + jnp.dot(a_ref[...], b_ref[...], preferred_element_type=F32)


def _matmul(a, b, out_dtype, tm, tn, name, bias=None, residual=None):
    m, k = a.shape
    n = b.shape[1]
    tm = min(tm, m)
    in_specs = [pl.BlockSpec((tm, k), lambda i, j: (i, 0)), pl.BlockSpec((k, tn), lambda i, j: (0, j))]
    args = [a, b]
    kern = _mm_kernel
    if bias is not None:
        in_specs.append(pl.BlockSpec((1, tn), lambda i, j: (0, j)))
        args.append(bias.reshape(1, n))
        kern = _mm_gate_kernel
    if residual is not None:
        in_specs.append(pl.BlockSpec((tm, tn), lambda i, j: (i, j)))
        args.append(residual)
        kern = _mm_res_kernel
    return pl.pallas_call(
        kern,
        out_shape=jax.ShapeDtypeStruct((m, n), out_dtype),
        grid=(m // tm, n // tn),
        in_specs=in_specs,
        out_specs=pl.BlockSpec((tm, tn), lambda i, j: (i, j)),
        compiler_params=_params(2),
        name=name,
    )(*args)


def _mix_kernel(a1_ref, b1_ref, a2_ref, b2_ref, g1_ref, g2_ref, o_ref):
    y1 = jnp.dot(a1_ref[...], b1_ref[...], preferred_element_type=F32)
    y2 = jnp.dot(a2_ref[...], b2_ref[...], preferred_element_type=F32)
    o_ref[...] = (g1_ref[...].astype(F32) * y1 + g2_ref[...].astype(F32) * y2).astype(o_ref.dtype)


def _branch_mix(o_rw, o_dn, wb_rw, wb_dn, gates, tm=1024, tn=512):
    m, k = o_rw.shape
    n = wb_rw.shape[1]
    tm = min(tm, m)
    goff = n // tn
    return pl.pallas_call(
        _mix_kernel,
        out_shape=jax.ShapeDtypeStruct((m, n), BF16),
        grid=(m // tm, n // tn),
        in_specs=[pl.BlockSpec((tm, k), lambda i, j: (i, 0)), pl.BlockSpec((k, tn), lambda i, j: (0, j)),
                  pl.BlockSpec((tm, k), lambda i, j: (i, 0)), pl.BlockSpec((k, tn), lambda i, j: (0, j)),
                  pl.BlockSpec((tm, tn), lambda i, j: (i, j)),
                  pl.BlockSpec((tm, tn), lambda i, j: (i, j + goff))],
        out_specs=pl.BlockSpec((tm, tn), lambda i, j: (i, j)),
        compiler_params=_params(2),
        name="branch_mix",
    )(o_rw, wb_rw, o_dn, wb_dn, gates, gates)


def _tri_incl():
    r = lax.broadcasted_iota(jnp.int32, (CHUNK, CHUNK), 0)
    c = lax.broadcasted_iota(jnp.int32, (CHUNK, CHUNK), 1)
    return (r >= c).astype(BF16)


def _eye_bf16(n):
    r = lax.broadcasted_iota(jnp.int32, (n, n), 0)
    c = lax.broadcasted_iota(jnp.int32, (n, n), 1)
    return (r == c).astype(BF16)


def _cumsum_rows(x):
    tri = _tri_incl()
    hi, lo = _split2(x)
    return (jnp.dot(tri, hi, preferred_element_type=F32) + jnp.dot(tri, lo, preferred_element_type=F32))


def _gdn_kernel(p_ref, halo_ref, cw_ref, alog_ref, dtb_ref, nw_ref, o_ref, ext_ref, state_ref):
    s = pl.program_id(1)

    @pl.when(s == 0)
    def _():
        state_ref[...] = jnp.zeros_like(state_ref)

    ext_ref[0:HALO_ROWS, :] = jnp.where(s == 0, 0.0, halo_ref[:, 0:DN_QKV])
    ext_ref[HALO_ROWS:HALO_ROWS + CHUNK, :] = p_ref[:, 0:DN_QKV]

    def conv_silu(col):
        cs = slice(col, col + DN_HEAD)
        acc = ext_ref[HALO_ROWS:HALO_ROWS + CHUNK, cs] * cw_ref[DN_CONV - 1:DN_CONV, cs]
        for j in range(DN_CONV - 1):
            r0 = HALO_ROWS - (DN_CONV - 1) + j
            acc = acc + ext_ref[r0:r0 + CHUNK, cs] * cw_ref[j:j + 1, cs]
        return _silu(acc)

    def l2norm(t):
        return t * lax.rsqrt(jnp.sum(t * t, axis=-1, keepdims=True) + NORM_EPS)

    small = p_ref[:, DN_OFF_SMALL:DN_OFF_SMALL + LORA_PAD]
    beta_all = _sigmoid(small)
    sp_in = small + dtb_ref[...]
    softplus = jnp.maximum(sp_in, 0.0) + jnp.log1p(jnp.exp(-jnp.abs(sp_in)))
    g_all = -jnp.exp(alog_ref[...]) * softplus
    gcum = _cumsum_rows(g_all)
    eye = _eye_bf16(LANES)
    ghi, glo = _split2(gcum)
    gcum_t = (lax.dot_general(eye, ghi, (((1,), (1,)), ((), ())), preferred_element_type=F32)
              + lax.dot_general(eye, glo, (((1,), (1,)), ((), ())), preferred_element_type=F32))

    r = lax.broadcasted_iota(jnp.int32, (CHUNK, CHUNK), 0)
    c = lax.broadcasted_iota(jnp.int32, (CHUNK, CHUNK), 1)
    causal = r >= c
    strict = r > c
    eye_c = jnp.where(r == c, 1.0, 0.0)
    nw = nw_ref[...]
    heads = range(DN_HEADS)

    q = [l2norm(conv_silu(h * DN_HEAD)) * (DN_HEAD ** -0.5) for h in heads]
    k = [l2norm(conv_silu(DN_KEY + h * DN_HEAD)) for h in heads]
    v = [conv_silu(2 * DN_KEY + h * DN_HEAD) for h in heads]
    gcol = [gcum[:, DN_HEADS + h:DN_HEADS + h + 1] for h in heads]
    glast = [g[CHUNK - 1:CHUNK, :] for g in gcol]
    decay = [jnp.where(causal, jnp.exp(jnp.where(causal, gcol[h] - gcum_t[DN_HEADS + h:DN_HEADS + h + 1, :], 0.0)),
                       0.0) for h in heads]
    beta = [beta_all[:, h:h + 1] for h in heads]
    kb = [k[h] * beta[h] for h in heads]
    eg = [jnp.exp(g) for g in gcol]
    kk = [_dot_nt(jnp.concatenate([kb[h], q[h]], axis=0), k[h]) for h in heads]
    k_tail_t = [_dot_nt(eye, k[h] * jnp.exp(glast[h] - gcol[h])) for h in heads]
    attn = [kk[h][CHUNK:] * decay[h] for h in heads]
    m = [-jnp.where(strict, kk[h][:CHUNK] * decay[h], 0.0) for h in heads]
    inv = [eye_c + m[h] for h in heads]
    m = [_dot(m[h], m[h]) for h in heads]
    for _ in range(4):
        both = [_dot(jnp.concatenate([m[h], inv[h]], axis=0), m[h]) for h in heads]
        inv = [inv[h] + both[h][CHUNK:] for h in heads]
        m = [both[h][:CHUNK] for h in heads]
    inv = [inv[h] + _dot(inv[h], m[h]) for h in heads]
    sol = [_dot(inv[h], jnp.concatenate([v[h] * beta[h], kb[h] * eg[h]], axis=1)) for h in heads]
    st = [state_ref[h] for h in heads]
    wq = [_dot(jnp.concatenate([sol[h][:, DN_HEAD:], q[h] * eg[h]], axis=0), st[h]) for h in heads]
    v_new = [sol[h][:, :DN_HEAD] - wq[h][:CHUNK] for h in heads]
    o = [wq[h][CHUNK:] + _dot(attn[h], v_new[h]) for h in heads]
    for h in heads:
        state_ref[h] = st[h] * jnp.exp(glast[h]) + _dot(k_tail_t[h], v_new[h])
    for h in heads:
        on = o[h] * lax.rsqrt(jnp.mean(o[h] * o[h], axis=-1, keepdims=True) + NORM_EPS) * nw
        z = p_ref[:, DN_OFF_Z + h * DN_HEAD:DN_OFF_Z + (h + 1) * DN_HEAD]
        o_ref[:, h * DN_HEAD:(h + 1) * DN_HEAD] = (on * _silu(z)).astype(o_ref.dtype)


def _gated_deltanet(p_dn, batch, seq, conv_w, a_log, dt_bias, norm_w):
    t = batch * seq
    ns = seq // CHUNK
    hb = CHUNK // HALO_ROWS
    alog = jnp.zeros((1, LORA_PAD), F32).at[0, DN_HEADS:2 * DN_HEADS].set(a_log)
    dtb = jnp.zeros((1, LORA_PAD), F32).at[0, DN_HEADS:2 * DN_HEADS].set(dt_bias)
    return pl.pallas_call(
        _gdn_kernel,
        out_shape=jax.ShapeDtypeStruct((t, DN_VAL), BF16),
        grid=(batch, ns),
        in_specs=[pl.BlockSpec((CHUNK, DN_PCOLS), lambda b, s: (b * ns + s, 0)),
                  pl.BlockSpec((HALO_ROWS, DN_PCOLS), lambda b, s: (jnp.maximum((b * ns + s) * hb - 1, 0), 0)),
                  pl.BlockSpec((DN_CONV, DN_QKV), lambda b, s: (0, 0)),
                  pl.BlockSpec((1, LORA_PAD), lambda b, s: (0, 0)),
                  pl.BlockSpec((1, LORA_PAD), lambda b, s: (0, 0)),
                  pl.BlockSpec((1, DN_HEAD), lambda b, s: (0, 0))],
        out_specs=pl.BlockSpec((CHUNK, DN_VAL), lambda b, s: (b * ns + s, 0)),
        scratch_shapes=[pltpu.VMEM((HALO_ROWS + CHUNK, DN_QKV), F32),
                        pltpu.VMEM((DN_HEADS, DN_HEAD, DN_HEAD), F32)],
        compiler_params=_params(2),
        name="gated_deltanet",
    )(p_dn, p_dn, conv_w, alog, dtb, norm_w.reshape(1, DN_HEAD))


def _rwkv_kernel(p_ref, halo_ref, mu_ref, w0_ref, a0_ref, kk_ref, ka_ref, rk_ref, lnw_ref, lnb_ref,
                 w2_ref, a2_ref, g2_ref, o_ref, state_ref):
    s = pl.program_id(1)

    @pl.when(s == 0)
    def _():
        state_ref[...] = jnp.zeros_like(state_ref)

    row1 = lax.broadcasted_iota(jnp.int32, (CHUNK, 1), 0)

    def shifted(lo, hi):
        cur = p_ref[:, lo:hi]
        last = jnp.where(s == 0, 0.0, halo_ref[HALO_ROWS - 1:HALO_ROWS, lo:hi])
        prev = jnp.where(row1 == 0, last, pltpu.roll(cur, 1, 0))
        return cur + (prev - cur) * mu_ref[:, lo:hi]

    wd = shifted(RW_OFF_WD, RW_OFF_WD + LORA_PAD)
    ad = shifted(RW_OFF_AD, RW_OFF_AD + LORA_PAD)
    gd = shifted(RW_OFF_GD, RW_OFF_GD + RW_GATE_LORA)
    log_w = -RW_DECAY_SCALE * _sigmoid(w0_ref[...] + _dot(jnp.tanh(wd), w2_ref[...]))
    a_all = _sigmoid(a0_ref[...] + _dot(ad, a2_ref[...]))
    g_all = _dot(_sigmoid(gd), g2_ref[...])

    lane = lax.broadcasted_iota(jnp.int32, (CHUNK, LANES), 1)
    row = lax.broadcasted_iota(jnp.int32, (CHUNK, LANES), 0)
    tcol = lane % RW_HEAD
    strict_ss = row > tcol
    incl_ss = row >= tcol
    eye_ss = jnp.where(row == tcol, 1.0, 0.0)
    r2 = lax.broadcasted_iota(jnp.int32, (LANES, LANES), 0)
    c2 = lax.broadcasted_iota(jnp.int32, (LANES, LANES), 1)
    bd_mask = (r2 // RW_HEAD) == (c2 // RW_HEAD)
    ones_bd = bd_mask.astype(BF16)
    head0 = lane < RW_HEAD
    eye = _eye_bf16(LANES)

    def seg_sum(x):
        s0 = jnp.sum(jnp.where(head0, x, 0.0), axis=-1, keepdims=True)
        s1 = jnp.sum(jnp.where(head0, 0.0, x), axis=-1, keepdims=True)
        return jnp.where(head0, s0, s1)

    def block_diag(x):
        return jnp.where(bd_mask, jnp.concatenate([x, x], axis=0), 0.0)

    def per_head_rows(x):
        return jnp.concatenate([jnp.where(head0, x, 0.0), jnp.where(head0, 0.0, x)], axis=0)

    def run_pairs(pairs):
        sl = {p: slice(p * LANES, (p + 1) * LANES) for p in pairs}
        r = {p: shifted(RW_OFF_R + p * LANES, RW_OFF_R + (p + 1) * LANES) for p in pairs}
        k = {p: shifted(RW_OFF_K + p * LANES, RW_OFF_K + (p + 1) * LANES) for p in pairs}
        v = {p: shifted(RW_OFF_V + p * LANES, RW_OFF_V + (p + 1) * LANES) for p in pairs}
        lw = {p: log_w[:, sl[p]] for p in pairs}
        a = {p: a_all[:, sl[p]] for p in pairs}
        kkr = {p: k[p] * kk_ref[:, sl[p]] for p in pairs}
        kss = {p: seg_sum(kkr[p] * kkr[p]) for p in pairs}
        gc = {p: _cumsum_rows(lw[p]) for p in pairs}
        kk = {p: kkr[p] * lax.rsqrt(kss[p] + NORM_EPS) for p in pairs}
        kh = {p: k[p] * (1.0 + (a[p] - 1.0) * ka_ref[:, sl[p]]) for p in pairs}
        b = {p: kk[p] * a[p] for p in pairs}
        bonus_s = {p: seg_sum(r[p] * kh[p] * rk_ref[:, sl[p]]) for p in pairs}
        glast = {p: gc[p][CHUNK - 1:CHUNK, :] for p in pairs}
        e_neg = {p: jnp.exp(-gc[p]) for p in pairs}
        e_tail = {p: jnp.exp(glast[p] - gc[p]) for p in pairs}
        x1 = {p: jnp.concatenate([kk[p] * jnp.exp(gc[p] - lw[p]), r[p] * jnp.exp(gc[p])], axis=0) for p in pairs}
        x3 = {p: jnp.concatenate([kh[p] * e_tail[p], b[p] * e_tail[p]], axis=0) for p in pairs}
        st = {p: state_ref[p] for p in pairs}
        akb = {p: _dot_nt(x1[p], jnp.concatenate([per_head_rows(kh[p] * e_neg[p]),
                                                  per_head_rows(b[p] * e_neg[p]), st[p]], axis=0)) for p in pairs}
        ak = {p: akb[p][:, 0:LANES] for p in pairs}
        ab = {p: akb[p][:, LANES:2 * LANES] for p in pairs}
        xh = {p: akb[p][:, 2 * LANES:3 * LANES] for p in pairs}
        bd_v = {p: block_diag(v[p]) for p in pairs}
        rhs = {p: xh[p][:CHUNK] + _dot(jnp.where(strict_ss, ak[p][:CHUNK], 0.0), bd_v[p]) for p in pairs}
        m = {p: -jnp.where(strict_ss, ab[p][:CHUNK], 0.0) for p in pairs}
        inv = {p: eye_ss + m[p] for p in pairs}
        m = {p: _dot(m[p], block_diag(m[p])) for p in pairs}
        for _ in range(4):
            both = {p: _dot(jnp.concatenate([m[p], inv[p]], axis=0), block_diag(m[p])) for p in pairs}
            inv = {p: inv[p] + both[p][CHUNK:] for p in pairs}
            m = {p: both[p][:CHUNK] for p in pairs}
        inv = {p: inv[p] + _dot(inv[p], block_diag(m[p])) for p in pairs}
        u = {p: _dot(inv[p], block_diag(rhs[p])) for p in pairs}
        o = {p: xh[p][CHUNK:] + _dot(jnp.concatenate([jnp.where(incl_ss, ak[p][CHUNK:], 0.0),
                                                      jnp.where(incl_ss, ab[p][CHUNK:], 0.0)], axis=1),
                                     jnp.concatenate([bd_v[p], -block_diag(u[p])], axis=0)) for p in pairs}
        vu_t = {p: jnp.concatenate([v[p], -u[p]], axis=0).T for p in pairs}
        for p in pairs:
            state_ref[p] = st[p] * jnp.exp(glast[p]) + jnp.where(bd_mask, _dot(vu_t[p], x3[p]), 0.0)
        oc = {p: o[p] - seg_sum(o[p]) * (1.0 / RW_HEAD) for p in pairs}
        var = {p: seg_sum(oc[p] * oc[p]) * (1.0 / RW_HEAD) for p in pairs}
        for p in pairs:
            on = oc[p] * lax.rsqrt(var[p] + RW_GN_EPS) * lnw_ref[:, sl[p]] + lnb_ref[:, sl[p]]
            o_ref[:, sl[p]] = ((on + bonus_s[p] * v[p]) * g_all[:, sl[p]]).astype(o_ref.dtype)

    n_pairs = RW_HEADS // 2
    for grp in range(n_pairs // RW_PAIR_GROUP):
        run_pairs(range(grp * RW_PAIR_GROUP, (grp + 1) * RW_PAIR_GROUP))


def _rwkv7(p_rw, batch, seq, mu, w0, w2, a0, a2, g2, k_k, k_a, r_k, ln_w, ln_b):
    t = batch * seq
    ns = seq // CHUNK
    hb = CHUNK // HALO_ROWS
    vec = lambda x: x.reshape(1, RW_WIDTH)
    cvec = pl.BlockSpec((1, RW_WIDTH), lambda b, s: (0, 0))
    return pl.pallas_call(
        _rwkv_kernel,
        out_shape=jax.ShapeDtypeStruct((t, RW_WIDTH), BF16),
        grid=(batch, ns),
        in_specs=[pl.BlockSpec((CHUNK, RW_PCOLS), lambda b, s: (b * ns + s, 0)),
                  pl.BlockSpec((HALO_ROWS, RW_PCOLS), lambda b, s: (jnp.maximum((b * ns + s) * hb - 1, 0), 0)),
                  pl.BlockSpec((1, RW_PCOLS), lambda b, s: (0, 0)),
                  cvec, cvec, cvec, cvec, cvec, cvec, cvec,
                  pl.BlockSpec((LORA_PAD, RW_WIDTH), lambda b, s: (0, 0)),
                  pl.BlockSpec((LORA_PAD, RW_WIDTH), lambda b, s: (0, 0)),
                  pl.BlockSpec((RW_GATE_LORA, RW_WIDTH), lambda b, s: (0, 0))],
        out_specs=pl.BlockSpec((CHUNK, RW_WIDTH), lambda b, s: (b * ns + s, 0)),
        scratch_shapes=[pltpu.VMEM((RW_HEADS // 2, LANES, LANES), F32)],
        compiler_params=_params(2),
        name="rwkv7",
    )(p_rw, p_rw, mu.reshape(1, RW_PCOLS), vec(w0), vec(a0), vec(k_k), vec(k_a), vec(r_k), vec(ln_w),
      vec(ln_b), w2, a2, g2)


PACK_ROWS = D_MODEL // (2 * LANES)
PACK_PITCH = PACK_ROWS + 2


def _pack_pair(a, b):
    ua = lax.bitcast_convert_type(a.astype(BF16).astype(F32), jnp.uint32)
    ub = lax.bitcast_convert_type(b.astype(BF16).astype(F32), jnp.uint32)
    return ua | lax.shift_right_logical(ub, jnp.uint32(16))


def _unpack_pair(u):
    a = lax.bitcast_convert_type(u & jnp.uint32(0xFFFF0000), F32)
    b = lax.bitcast_convert_type(lax.shift_left(u, jnp.uint32(16)), F32)
    return a, b


def _store_packed(ref, rows, x):
    half = D_MODEL // 2
    for j in range(PACK_ROWS):
        ref[pl.ds(j, rows, stride=PACK_PITCH), :] = _pack_pair(x[:, j * LANES:(j + 1) * LANES],
                                                               x[:, half + j * LANES:half + (j + 1) * LANES])
    for j in range(PACK_ROWS, PACK_PITCH):
        ref[pl.ds(j, rows, stride=PACK_PITCH), :] = jnp.zeros((rows, LANES), jnp.uint32)


def _router_kernel(x_ref, g_ref, wr_ref, br_ref, xpk_ref, eid_ref, wgt_ref):
    x = x_ref[...]
    xn = x * lax.rsqrt(jnp.mean(x * x, axis=-1, keepdims=True) + NORM_EPS) * g_ref[...]
    _store_packed(xpk_ref, x.shape[0], xn)
    xh, xl = _split2(xn)
    wh, wl = _split2(wr_ref[...])
    logits = (jnp.dot(xh, wh, preferred_element_type=F32) + jnp.dot(xh, wl, preferred_element_type=F32)
              + jnp.dot(xl, wh, preferred_element_type=F32)) + br_ref[...]
    tm = logits.shape[0]
    lane = lax.broadcasted_iota(jnp.int32, (tm, LANES), 1)
    neg = -1e30
    big = 4 * LANES

    def first_argmax(vals):
        mx = jnp.max(vals, axis=-1, keepdims=True)
        idx = jnp.min(jnp.where(vals == mx, lane, big), axis=-1, keepdims=True)
        return mx, idx

    gl = jnp.where(lane < N_GROUPS, logits, neg)
    gmax, gi = first_argmax(gl)
    gp = 1.0 / jnp.sum(jnp.where(lane < N_GROUPS, jnp.exp(gl - gmax), 0.0), axis=-1, keepdims=True)
    e_lo = N_GROUPS + gi * EXPERTS_PER_GROUP
    el = jnp.where((lane >= e_lo) & (lane < e_lo + EXPERTS_PER_GROUP), logits, neg)
    m1, i1 = first_argmax(el)
    m2, i2 = first_argmax(jnp.where(lane == i1, neg, el))
    e2 = jnp.exp(m2 - m1)
    w1 = gp / (1.0 + e2)
    w2 = gp * e2 / (1.0 + e2)
    eid_ref[...] = jnp.where(lane == 0, i1 - N_GROUPS, jnp.where(lane == 1, i2 - N_GROUPS, 0))
    wgt_ref[...] = jnp.where(lane == 0, w1, jnp.where(lane == 1, w2, 0.0))


def _router(x_mid, norm_g, wr, br, tm=256):
    t, d = x_mid.shape
    tm = min(tm, t)
    return pl.pallas_call(
        _router_kernel,
        out_shape=(jax.ShapeDtypeStruct((t * PACK_PITCH, LANES), jnp.uint32),
                   jax.ShapeDtypeStruct((t, LANES), jnp.int32),
                   jax.ShapeDtypeStruct((t, LANES), F32)),
        grid=(t // tm,),
        in_specs=[pl.BlockSpec((tm, d), lambda i: (i, 0)), pl.BlockSpec((1, d), lambda i: (0, 0)),
                  pl.BlockSpec((d, LANES), lambda i: (0, 0)), pl.BlockSpec((1, LANES), lambda i: (0, 0))],
        out_specs=(pl.BlockSpec((tm * PACK_PITCH, LANES), lambda i: (i, 0)),
                   pl.BlockSpec((tm, LANES), lambda i: (i, 0)), pl.BlockSpec((tm, LANES), lambda i: (i, 0))),
        compiler_params=_params(1),
        name="router",
    )(x_mid, norm_g.reshape(1, d), wr, br)


DMA_GROUP = 8


def _expert_changed(be_ref, i):
    return jnp.logical_or(i == 0, be_ref[i] != be_ref[jnp.maximum(i - 1, 0)])


def _moe_up_kernel(be_ref, nv_ref, row_ref, x_hbm, w1_ref, w3_ref, h_ref, xbuf, xb, wb1, wb3, sem):
    i = pl.program_id(0)
    j = pl.program_id(1)
    nb = pl.num_programs(0)
    slot = i % 2
    half = D_MODEL // 2
    rows = h_ref.shape[0]

    def gather(blk, sl, wait):
        n = nv_ref[blk]
        ng = lax.shift_right_logical(n, 3)

        def row_copy(r):
            src = row_ref[blk * rows + r]
            return pltpu.make_async_copy(x_hbm.at[pl.ds(src, PACK_ROWS)],
                                         xbuf.at[sl, pl.ds(r * PACK_PITCH, PACK_ROWS)], sem.at[sl])

        def group(g, carry):
            if wait:
                n_rows = DMA_GROUP * PACK_ROWS
                pltpu.make_async_copy(x_hbm.at[pl.ds(0, n_rows)], xbuf.at[sl, pl.ds(0, n_rows)],
                                      sem.at[sl]).wait()
            else:
                for q in range(DMA_GROUP):
                    row_copy(g * DMA_GROUP + q).start()
            return carry

        def single(r, carry):
            if wait:
                row_copy(r).wait()
            else:
                row_copy(r).start()
            return carry

        lax.fori_loop(0, ng, group, 0)
        lax.fori_loop(ng * DMA_GROUP, n, single, 0)

    @pl.when(jnp.logical_and(i == 0, j == 0))
    def _():
        xbuf[...] = jnp.zeros_like(xbuf)
        gather(0, 0, False)

    @pl.when(jnp.logical_and(j == 0, i + 1 < nb))
    def _():
        gather(jnp.minimum(i + 1, nb - 1), 1 - slot, False)

    @pl.when(j == 0)
    def _():
        gather(i, slot, True)

    @pl.when(jnp.logical_and(j == 0, nv_ref[i] > 0))
    def _():
        xs = xbuf.at[slot]
        for c in range(PACK_ROWS):
            hi, lo = _unpack_pair(xs[pl.ds(c, rows, stride=PACK_PITCH), :])
            xb[:, c * LANES:(c + 1) * LANES] = hi.astype(BF16)
            xb[:, half + c * LANES:half + (c + 1) * LANES] = lo.astype(BF16)

    @pl.when(jnp.logical_and(_expert_changed(be_ref, i), nv_ref[i] > 0))
    def _():
        wb1[j] = w1_ref[0].astype(BF16)
        wb3[j] = w3_ref[0].astype(BF16)

    def up_rows(m):
        x = xb[0:m, :]
        h1 = jnp.dot(x, wb1[j], preferred_element_type=F32)
        h3 = jnp.dot(x, wb3[j], preferred_element_type=F32)
        h_ref[0:m, :] = (_silu(h1) * h3).astype(h_ref.dtype)

    @pl.when(nv_ref[i] > rows // 2)
    def _():
        up_rows(rows)

    @pl.when(jnp.logical_and(nv_ref[i] > 0, nv_ref[i] <= rows // 2))
    def _():
        up_rows(rows // 2)
        h_ref[rows // 2:, :] = jnp.zeros((rows - rows // 2, h_ref.shape[1]), h_ref.dtype)

    @pl.when(nv_ref[i] == 0)
    def _():
        h_ref[...] = jnp.zeros_like(h_ref)


def _moe_down_kernel(be_ref, nv_ref, dst_ref, h_ref, w2_ref, y_hbm, ybuf, wb2, sem):
    i = pl.program_id(0)
    nb = pl.num_programs(0)
    slot = i % 2
    rows = h_ref.shape[0]

    def row_copy(blk, sl, r):
        d = dst_ref[blk * rows + r]
        return pltpu.make_async_copy(ybuf.at[sl, pl.ds(r * PACK_PITCH, PACK_PITCH)],
                                     y_hbm.at[pl.ds(d, PACK_PITCH)], sem.at[sl])

    def scatter(blk, sl, wait):
        n = nv_ref[blk]
        ng = lax.shift_right_logical(n, 3)

        def group(g, carry):
            if wait:
                n_rows = DMA_GROUP * PACK_PITCH
                r0 = pl.multiple_of(g * n_rows, 8)
                pltpu.make_async_copy(ybuf.at[sl, pl.ds(r0, n_rows)], y_hbm.at[pl.ds(0, n_rows)],
                                      sem.at[sl]).wait()
            else:
                for j in range(DMA_GROUP):
                    row_copy(blk, sl, g * DMA_GROUP + j).start()
            return carry

        def single(r, carry):
            cp = row_copy(blk, sl, r)
            if wait:
                cp.wait()
            else:
                cp.start()
            return carry

        lax.fori_loop(0, ng, group, 0)
        lax.fori_loop(ng * DMA_GROUP, n, single, 0)

    prev2 = jnp.maximum(i - 2, 0)

    @pl.when(i >= 2)
    def _():
        scatter(prev2, slot, True)

    @pl.when(nv_ref[i] > 0)
    def _():
        @pl.when(_expert_changed(be_ref, i))
        def _():
            wb2[...] = w2_ref[0].astype(BF16)

        _store_packed(ybuf.at[slot], rows, jnp.dot(h_ref[...], wb2[...], preferred_element_type=F32))
        scatter(i, slot, False)

    prev1 = jnp.maximum(i - 1, 0)

    @pl.when(jnp.logical_and(i == nb - 1, i >= 1))
    def _():
        scatter(prev1, 1 - slot, True)

    @pl.when(i == nb - 1)
    def _():
        scatter(i, slot, True)


def _moe_experts(xpk, t, eid, w1, w3, w2):
    d = D_MODEL
    a = t * TOP_K
    e_flat = eid.reshape(a)
    tok_flat = jnp.repeat(jnp.arange(t, dtype=jnp.int32), TOP_K)
    dst_flat = (jnp.arange(a, dtype=jnp.int32) % TOP_K) * t + tok_flat
    onehot = (e_flat[:, None] == jnp.arange(N_EXPERTS, dtype=jnp.int32)[None, :]).astype(jnp.int32)
    csum = jnp.cumsum(onehot, axis=0)
    rank = jnp.sum(csum * onehot, axis=1) - 1
    counts = csum[-1]
    pcounts = (counts + MOE_PAD - 1) // MOE_PAD * MOE_PAD
    pend = jnp.cumsum(pcounts)
    poffs = pend - pcounts
    dest = poffs[e_flat] + rank
    p_rows = a + N_EXPERTS * MOE_PAD
    slot_dst = jnp.zeros((p_rows,), jnp.int32).at[dest].set(dst_flat)
    slot_src = (slot_dst % t) * PACK_PITCH
    slot_dst = slot_dst * PACK_PITCH

    def blocks(rows):
        starts = jnp.arange(p_rows // rows, dtype=jnp.int32) * rows
        block_e = jnp.minimum(jnp.searchsorted(pend, starts, side='right'), N_EXPERTS - 1).astype(jnp.int32)
        block_v = jnp.clip(poffs[block_e] + counts[block_e] - starts, 0, rows).astype(jnp.int32)
        last_e = block_e[jnp.maximum(jnp.sum(starts < pend[-1]) - 1, 0)]
        return jnp.where(starts < pend[-1], block_e, last_e), block_v

    up_e, up_v = blocks(UP_ROWS)

    def w_index(i, j, be, bv, tk):
        fetch = jnp.logical_and(_expert_changed(be, i), bv[i] > 0)
        return be[i], 0, jnp.where(fetch, j, UP_TILES - 1)

    h = pl.pallas_call(
        _moe_up_kernel,
        out_shape=jax.ShapeDtypeStruct((p_rows, D_EXPERT), BF16),
        grid_spec=pltpu.PrefetchScalarGridSpec(
            num_scalar_prefetch=3,
            grid=(p_rows // UP_ROWS, UP_TILES),
            in_specs=[pl.BlockSpec(memory_space=pl.ANY), pl.BlockSpec((1, d, UP_TILE), w_index),
                      pl.BlockSpec((1, d, UP_TILE), w_index)],
            out_specs=pl.BlockSpec((UP_ROWS, UP_TILE), lambda i, j, be, bv, tk: (i, j)),
            scratch_shapes=[pltpu.VMEM((2, UP_ROWS * PACK_PITCH, LANES), jnp.uint32),
                            pltpu.VMEM((UP_ROWS, d), BF16), pltpu.VMEM((UP_TILES, d, UP_TILE), BF16),
                            pltpu.VMEM((UP_TILES, d, UP_TILE), BF16), pltpu.SemaphoreType.DMA((2,))]),
        compiler_params=_params(2),
        name="moe_up",
    )(up_e, up_v, slot_src, xpk, w1, w3)

    down_e, down_v = blocks(DOWN_ROWS)
    y = pl.pallas_call(
        _moe_down_kernel,
        out_shape=jax.ShapeDtypeStruct((TOP_K * t * PACK_PITCH, LANES), jnp.uint32),
        grid_spec=pltpu.PrefetchScalarGridSpec(
            num_scalar_prefetch=3,
            grid=(p_rows // DOWN_ROWS,),
            in_specs=[pl.BlockSpec((DOWN_ROWS, D_EXPERT), lambda i, be, bv, ds: (i, 0)),
                      pl.BlockSpec((1, D_EXPERT, d), lambda i, be, bv, ds: (be[i], 0, 0))],
            out_specs=pl.BlockSpec(memory_space=pl.ANY),
            scratch_shapes=[pltpu.VMEM((2, DOWN_ROWS * PACK_PITCH, LANES), jnp.uint32),
                            pltpu.VMEM((D_EXPERT, d), BF16), pltpu.SemaphoreType.DMA((2,))]),
        compiler_params=_params(1),
        name="moe_down",
    )(down_e, down_v, slot_dst, h, w2)
    return y


def _final_kernel(x_ref, y0_ref, y1_ref, wgt_ref, g_ref, o_ref):
    tm = x_ref.shape[0]
    half = D_MODEL // 2
    w0 = wgt_ref[:, 0:1]
    w1 = wgt_ref[:, 1:2]
    ss = jnp.zeros((tm, 1), F32)
    for j in range(PACK_ROWS):
        a0, b0 = _unpack_pair(y0_ref[pl.ds(j, tm, stride=PACK_PITCH), :])
        a1, b1 = _unpack_pair(y1_ref[pl.ds(j, tm, stride=PACK_PITCH), :])
        ca = slice(j * LANES, (j + 1) * LANES)
        cb = slice(half + j * LANES, half + (j + 1) * LANES)
        za = x_ref[:, ca] + w0 * a0 + w1 * a1
        zb = x_ref[:, cb] + w0 * b0 + w1 * b1
        o_ref[:, ca] = za
        o_ref[:, cb] = zb
        ss = ss + jnp.sum(za * za, axis=-1, keepdims=True) + jnp.sum(zb * zb, axis=-1, keepdims=True)
    o_ref[...] = o_ref[...] * lax.rsqrt(ss * (1.0 / D_MODEL) + NORM_EPS) * g_ref[...]


def _final(x_mid, y, wgt, g, tm=256):
    t, d = x_mid.shape
    tm = min(tm, t)
    nblk = t // tm
    return pl.pallas_call(
        _final_kernel,
        out_shape=jax.ShapeDtypeStruct((t, d), F32),
        grid=(nblk,),
        in_specs=[pl.BlockSpec((tm, d), lambda i: (i, 0)),
                  pl.BlockSpec((tm * PACK_PITCH, LANES), lambda i: (i, 0)),
                  pl.BlockSpec((tm * PACK_PITCH, LANES), lambda i: (i + nblk, 0)),
                  pl.BlockSpec((tm, LANES), lambda i: (i, 0)), pl.BlockSpec((1, d), lambda i: (0, 0))],
        out_specs=pl.BlockSpec((tm, d), lambda i: (i, 0)),
        compiler_params=_params(1),
        name="final_combine",
    )(x_mid, y, y, wgt, g.reshape(1, d))


def _pad_cols(w, n):
    return jnp.pad(w, ((0, 0), (0, n - w.shape[1])))


def _pad_rows(w, n):
    return jnp.pad(w, ((0, n - w.shape[0]), (0, 0)))


def _layer(x2, batch, seq, norm1_g, w_in, rw_mu, rw_w0, rw_w2, rw_a0, rw_a2, rw_g2, rw_k_k, rw_k_a, rw_r_k,
           rw_ln_w, rw_ln_b, dn_conv_w, dn_a_log, dn_dt_bias, dn_norm_w, gate_b, w_branch, w_out, norm2_g,
           moe_gr_w, moe_gr_b, moe_er_w, moe_er_b, moe_w1, moe_w3, moe_w2):
    sizes = (RW_WIDTH, RW_DECAY_LORA, RW_WIDTH, RW_WIDTH, RW_A_LORA, RW_GATE_LORA,
             DN_QKV, 2 * DN_HEADS, DN_VAL, 2 * D_MODEL)
    offs = [0]
    for sz in sizes:
        offs.append(offs[-1] + sz)
    cols = lambda w, i: w[..., offs[i]:offs[i + 1]]
    w_bf = w_in.astype(BF16)
    w_rw = jnp.concatenate([cols(w_bf, 0), _pad_cols(cols(w_bf, 1), LORA_PAD), cols(w_bf, 2), cols(w_bf, 3),
                            _pad_cols(cols(w_bf, 4), LORA_PAD), cols(w_bf, 5)], axis=1)
    w_dn = _pad_cols(jnp.concatenate([cols(w_bf, 6), _pad_cols(cols(w_bf, 7), LORA_PAD), cols(w_bf, 8)], axis=1),
                     DN_PCOLS)
    w_gate = cols(w_bf, 9)
    mu2 = rw_mu.reshape(1, -1)
    mu = jnp.concatenate([cols(mu2, 0), _pad_cols(cols(mu2, 1), LORA_PAD), cols(mu2, 2), cols(mu2, 3),
                          _pad_cols(cols(mu2, 4), LORA_PAD), cols(mu2, 5)], axis=1)

    h = _rmsnorm(x2, norm1_g, BF16)
    p_rw = _matmul(h, w_rw, F32, 1024, 512, "proj_rwkv")
    p_dn = _matmul(h, w_dn, F32, 1024, 3 * MXU_WIDTH, "proj_gdn")
    gates = _matmul(h, w_gate, BF16, 1024, 1024, "proj_gate", bias=gate_b)

    o_rw = _rwkv7(p_rw, batch, seq, mu, rw_w0, _pad_rows(rw_w2, LORA_PAD).astype(BF16), rw_a0,
                  _pad_rows(rw_a2, LORA_PAD).astype(BF16), rw_g2.astype(BF16), rw_k_k, rw_k_a,
                  rw_r_k.reshape(-1), rw_ln_w, rw_ln_b)
    o_dn = _gated_deltanet(p_dn, batch, seq, dn_conv_w, dn_a_log, dn_dt_bias, dn_norm_w)

    mix = _branch_mix(o_rw, o_dn, w_branch[:RW_WIDTH].astype(BF16), w_branch[RW_WIDTH:].astype(BF16), gates,
                      tn=1024)
    x_mid = _matmul(mix, w_out.astype(BF16), F32, 1024, 1024, "proj_out", residual=x2)

    wr = _pad_cols(jnp.concatenate([moe_gr_w, moe_er_w], axis=1), LANES)
    br = _pad_cols(jnp.concatenate([moe_gr_b, moe_er_b]).reshape(1, -1), LANES)
    xpk, eid, wgt = _router(x_mid, norm2_g, wr, br)
    y = _moe_experts(xpk, x_mid.shape[0], eid[:, :TOP_K], moe_w1, moe_w3, moe_w2)
    return x_mid, y, wgt


def kernel(x, norm1_g, w_in, rw_mu, rw_w0, rw_w2, rw_a0, rw_a2, rw_g2, rw_k_k, rw_k_a, rw_r_k, rw_ln_w, rw_ln_b, dn_conv_w, dn_a_log, dn_dt_bias, dn_norm_w, gate_b, w_branch, w_out, norm2_g, moe_gr_w, moe_gr_b, moe_er_w, moe_er_b, moe_w1, moe_w3, moe_w2, final_g):
    batch, seq, d = x.shape
    assert norm1_g.shape[0] == 1, "single-layer block"
    x_mid, y, wgt = _layer(
        x.reshape(batch * seq, d), batch, seq, norm1_g[0], w_in[0], rw_mu[0], rw_w0[0], rw_w2[0], rw_a0[0],
        rw_a2[0], rw_g2[0], rw_k_k[0], rw_k_a[0], rw_r_k[0], rw_ln_w[0], rw_ln_b[0], dn_conv_w[0], dn_a_log[0],
        dn_dt_bias[0], dn_norm_w[0], gate_b[0], w_branch[0], w_out[0], norm2_g[0], moe_gr_w[0], moe_gr_b[0],
        moe_er_w[0], moe_er_b[0], moe_w1[0], moe_w3[0], moe_w2[0])
    return _final(x_mid, y, wgt, final_g).reshape(batch, seq, d)
```

```python
import functools

import jax
import jax.numpy as jnp
from jax import lax
from jax.experimental import pallas as pl
from jax.experimental.pallas import tpu as pltpu

F32 = jnp.float32
BF16 = jnp.bfloat16

D_MODEL = 4096
RW_HEADS = 32
RW_HEAD = 64
RW_WIDTH = RW_HEADS * RW_HEAD
RW_DECAY_LORA = 96
RW_A_LORA = 96
RW_GATE_LORA = 256
RW_GN_EPS = 64e-5
RW_DECAY_SCALE = 0.606531
DN_HEADS = 16
DN_HEAD = 128
DN_KEY = DN_HEADS * DN_HEAD
DN_VAL = DN_HEADS * DN_HEAD
DN_QKV = 2 * DN_KEY + DN_VAL
DN_CONV = 4
N_GROUPS = 8
EXPERTS_PER_GROUP = 8
N_EXPERTS = N_GROUPS * EXPERTS_PER_GROUP
TOP_K = 2
D_EXPERT = 768
NORM_EPS = 1e-6

LANES = 128
HALO_ROWS = 8
CHUNK = 64
RW_PAIR_GROUP = 16
LORA_PAD = 128
RW_PCOLS = 3 * RW_WIDTH + 2 * LORA_PAD + RW_GATE_LORA
MXU_WIDTH = 256
DN_PCOLS = -(-(DN_QKV + LORA_PAD + DN_VAL) // (3 * MXU_WIDTH)) * (3 * MXU_WIDTH)
RW_OFF_R = 0
RW_OFF_WD = RW_WIDTH
RW_OFF_K = RW_OFF_WD + LORA_PAD
RW_OFF_V = RW_OFF_K + RW_WIDTH
RW_OFF_AD = RW_OFF_V + RW_WIDTH
RW_OFF_GD = RW_OFF_AD + LORA_PAD
DN_OFF_SMALL = DN_QKV
DN_OFF_Z = DN_QKV + LORA_PAD
MOE_PAD = 512
UP_ROWS = 512
DOWN_ROWS = 256
UP_TILES = 3
UP_TILE = D_EXPERT // UP_TILES
VMEM_LIMIT = 56 * 1024 * 1024


def _params(n_axes):
    return pltpu.CompilerParams(dimension_semantics=("arbitrary",) * n_axes,
                                vmem_limit_bytes=VMEM_LIMIT)


def _dot(a, b):
    return jnp.dot(a.astype(BF16), b.astype(BF16), preferred_element_type=F32)


def _dot_nt(a, b):
    return lax.dot_general(a.astype(BF16), b.astype(BF16), (((1,), (1,)), ((), ())),
                           preferred_element_type=F32)


def _split2(x):
    hi = x.astype(BF16)
    lo = (x - hi.astype(F32)).astype(BF16)
    return hi, lo


def _sigmoid(x):
    return 1.0 / (1.0 + jnp.exp(-x))


def _silu(x):
    return x * _sigmoid(x)


def _rmsnorm_kernel(x_ref, g_ref, o_ref):
    x = x_ref[...]
    ms = jnp.mean(x * x, axis=-1, keepdims=True)
    o_ref[...] = (x * lax.rsqrt(ms + NORM_EPS) * g_ref[...]).astype(o_ref.dtype)


def _rmsnorm(x, g, out_dtype, tm=256):
    t, d = x.shape
    tm = min(tm, t)
    return pl.pallas_call(
        _rmsnorm_kernel,
        out_shape=jax.ShapeDtypeStruct((t, d), out_dtype),
        grid=(t // tm,),
        in_specs=[pl.BlockSpec((tm, d), lambda i: (i, 0)), pl.BlockSpec((1, d), lambda i: (0, 0))],
        out_specs=pl.BlockSpec((tm, d), lambda i: (i, 0)),
        compiler_params=_params(1),
        name="rmsnorm",
    )(x, g.reshape(1, d))


def _mm_kernel(a_ref, b_ref, o_ref):
    o_ref[...] = jnp.dot(a_ref[...], b_ref[...], preferred_element_type=F32).astype(o_ref.dtype)


def _mm_gate_kernel(a_ref, b_ref, bias_ref, o_ref):
    acc = jnp.dot(a_ref[...], b_ref[...], preferred_element_type=F32)
    o_ref[...] = _sigmoid(acc + bias_ref[...]).astype(o_ref.dtype)


def _mm_res_kernel(a_ref, b_ref, x_ref, o_ref):
    o_ref[...] = x_ref[...] + jnp.dot(a_ref[...], b_ref[...], preferred_element_type=F32)


def _matmul(a, b, out_dtype, tm, tn, name, bias=None, residual=None):
    m, k = a.shape
    n = b.shape[1]
    tm = min(tm, m)
    in_specs = [pl.BlockSpec((tm, k), lambda i, j: (i, 0)), pl.BlockSpec((k, tn), lambda i, j: (0, j))]
    args = [a, b]
    kern = _mm_kernel
    if bias is not None:
        in_specs.append(pl.BlockSpec((1, tn), lambda i, j: (0, j)))
        args.append(bias.reshape(1, n))
        kern = _mm_gate_kernel
    if residual is not None:
        in_specs.append(pl.BlockSpec((tm, tn), lambda i, j: (i, j)))
        args.append(residual)
        kern = _mm_res_kernel
    return pl.pallas_call(
        kern,
        out_shape=jax.ShapeDtypeStruct((m, n), out_dtype),
        grid=(m // tm, n // tn),
        in_specs=in_specs,
        out_specs=pl.BlockSpec((tm, tn), lambda i, j: (i, j)),
        compiler_params=_params(2),
        name=name,
    )(*args)


def _mix_kernel(a1_ref, b1_ref, a2_ref, b2_ref, g1_ref, g2_ref, o_ref):
    y1 = jnp.dot(a1_ref[...], b1_ref[...], preferred_element_type=F32)
    y2 = jnp.dot(a2_ref[...], b2_ref[...], preferred_element_type=F32)
    o_ref[...] = (g1_ref[...].astype(F32) * y1 + g2_ref[...].astype(F32) * y2).astype(o_ref.dtype)


def _branch_mix(o_rw, o_dn, wb_rw, wb_dn, gates, tm=1024, tn=512):
    m, k = o_rw.shape
    n = wb_rw.shape[1]
    tm = min(tm, m)
    goff = n // tn
    return pl.pallas_call(
        _mix_kernel,
        out_shape=jax.ShapeDtypeStruct((m, n), BF16),
        grid=(m // tm, n // tn),
        in_specs=[pl.BlockSpec((tm, k), lambda i, j: (i, 0)), pl.BlockSpec((k, tn), lambda i, j: (0, j)),
                  pl.BlockSpec((tm, k), lambda i, j: (i, 0)), pl.BlockSpec((k, tn), lambda i, j: (0, j)),
                  pl.BlockSpec((tm, tn), lambda i, j: (i, j)),
                  pl.BlockSpec((tm, tn), lambda i, j: (i, j + goff))],
        out_specs=pl.BlockSpec((tm, tn), lambda i, j: (i, j)),
        compiler_params=_params(2),
        name="branch_mix",
    )(o_rw, wb_rw, o_dn, wb_dn, gates, gates)


def _tri_incl():
    r = lax.broadcasted_iota(jnp.int32, (CHUNK, CHUNK), 0)
    c = lax.broadcasted_iota(jnp.int32, (CHUNK, CHUNK), 1)
    return (r >= c).astype(BF16)


def _eye_bf16(n):
    r = lax.broadcasted_iota(jnp.int32, (n, n), 0)
    c = lax.broadcasted_iota(jnp.int32, (n, n), 1)
    return (r == c).astype(BF16)


def _cumsum_rows(x):
    tri = _tri_incl()
    hi, lo = _split2(x)
    return (jnp.dot(tri, hi, preferred_element_type=F32) + jnp.dot(tri, lo, preferred_element_type=F32))


def _gdn_kernel(p_ref, halo_ref, cw_ref, alog_ref, dtb_ref, nw_ref, o_ref, ext_ref, state_ref):
    s = pl.program_id(1)

    @pl.when(s == 0)
    def _():
        state_ref[...] = jnp.zeros_like(state_ref)

    ext_ref[0:HALO_ROWS, :] = jnp.where(s == 0, 0.0, halo_ref[:, 0:DN_QKV])
    ext_ref[HALO_ROWS:HALO_ROWS + CHUNK, :] = p_ref[:, 0:DN_QKV]

    def conv_silu(col):
        cs = slice(col, col + DN_HEAD)
        acc = ext_ref[HALO_ROWS:HALO_ROWS + CHUNK, cs] * cw_ref[DN_CONV - 1:DN_CONV, cs]
        for j in range(DN_CONV - 1):
            r0 = HALO_ROWS - (DN_CONV - 1) + j
            acc = acc + ext_ref[r0:r0 + CHUNK, cs] * cw_ref[j:j + 1, cs]
        return _silu(acc)

    def l2norm(t):
        return t * lax.rsqrt(jnp.sum(t * t, axis=-1, keepdims=True) + NORM_EPS)

    small = p_ref[:, DN_OFF_SMALL:DN_OFF_SMALL + LORA_PAD]
    beta_all = _sigmoid(small)
    sp_in = small + dtb_ref[...]
    softplus = jnp.maximum(sp_in, 0.0) + jnp.log1p(jnp.exp(-jnp.abs(sp_in)))
    g_all = -jnp.exp(alog_ref[...]) * softplus
    gcum = _cumsum_rows(g_all)
    eye = _eye_bf16(LANES)
    ghi, glo = _split2(gcum)
    gcum_t = (lax.dot_general(eye, ghi, (((1,), (1,)), ((), ())), preferred_element_type=F32)
              + lax.dot_general(eye, glo, (((1,), (1,)), ((), ())), preferred_element_type=F32))

    r = lax.broadcasted_iota(jnp.int32, (CHUNK, CHUNK), 0)
    c = lax.broadcasted_iota(jnp.int32, (CHUNK, CHUNK), 1)
    causal = r >= c
    strict = r > c
    eye_c = jnp.where(r == c, 1.0, 0.0)
    nw = nw_ref[...]
    heads = range(DN_HEADS)

    q = [l2norm(conv_silu(h * DN_HEAD)) * (DN_HEAD ** -0.5) for h in heads]
    k = [l2norm(conv_silu(DN_KEY + h * DN_HEAD)) for h in heads]
    v = [conv_silu(2 * DN_KEY + h * DN_HEAD) for h in heads]
    gcol = [gcum[:, DN_HEADS + h:DN_HEADS + h + 1] for h in heads]
    glast = [g[CHUNK - 1:CHUNK, :] for g in gcol]
    decay = [jnp.where(causal, jnp.exp(jnp.where(causal, gcol[h] - gcum_t[DN_HEADS + h:DN_HEADS + h + 1, :], 0.0)),
                       0.0) for h in heads]
    beta = [beta_all[:, h:h + 1] for h in heads]
    kb = [k[h] * beta[h] for h in heads]
    eg = [jnp.exp(g) for g in gcol]
    kk = [_dot_nt(jnp.concatenate([kb[h], q[h]], axis=0), k[h]) for h in heads]
    k_tail_t = [_dot_nt(eye, k[h] * jnp.exp(glast[h] - gcol[h])) for h in heads]
    attn = [kk[h][CHUNK:] * decay[h] for h in heads]
    m = [-jnp.where(strict, kk[h][:CHUNK] * decay[h], 0.0) for h in heads]
    inv = [eye_c + m[h] for h in heads]
    m = [_dot(m[h], m[h]) for h in heads]
    for _ in range(4):
        both = [_dot(jnp.concatenate([m[h], inv[h]], axis=0), m[h]) for h in heads]
        inv = [inv[h] + both[h][CHUNK:] for h in heads]
        m = [both[h][:CHUNK] for h in heads]
    inv = [inv[h] + _dot(inv[h], m[h]) for h in heads]
    sol = [_dot(inv[h], jnp.concatenate([v[h] * beta[h], kb[h] * eg[h]], axis=1)) for h in heads]
    st = [state_ref[h] for h in heads]
    wq = [_dot(jnp.concatenate([sol[h][:, DN_HEAD:], q[h] * eg[h]], axis=0), st[h]) for h in heads]
    v_new = [sol[h][:, :DN_HEAD] - wq[h][:CHUNK] for h in heads]
    o = [wq[h][CHUNK:] + _dot(attn[h], v_new[h]) for h in heads]
    for h in heads:
        state_ref[h] = st[h] * jnp.exp(glast[h]) + _dot(k_tail_t[h], v_new[h])
    for h in heads:
        on = o[h] * lax.rsqrt(jnp.mean(o[h] * o[h], axis=-1, keepdims=True) + NORM_EPS) * nw
        z = p_ref[:, DN_OFF_Z + h * DN_HEAD:DN_OFF_Z + (h + 1) * DN_HEAD]
        o_ref[:, h * DN_HEAD:(h + 1) * DN_HEAD] = (on * _silu(z)).astype(o_ref.dtype)


def _gated_deltanet(p_dn, batch, seq, conv_w, a_log, dt_bias, norm_w):
    t = batch * seq
    ns = seq // CHUNK
    hb = CHUNK // HALO_ROWS
    alog = jnp.zeros((1, LORA_PAD), F32).at[0, DN_HEADS:2 * DN_HEADS].set(a_log)
    dtb = jnp.zeros((1, LORA_PAD), F32).at[0, DN_HEADS:2 * DN_HEADS].set(dt_bias)
    return pl.pallas_call(
        _gdn_kernel,
        out_shape=jax.ShapeDtypeStruct((t, DN_VAL), BF16),
        grid=(batch, ns),
        in_specs=[pl.BlockSpec((CHUNK, DN_PCOLS), lambda b, s: (b * ns + s, 0)),
                  pl.BlockSpec((HALO_ROWS, DN_PCOLS), lambda b, s: (jnp.maximum((b * ns + s) * hb - 1, 0), 0)),
                  pl.BlockSpec((DN_CONV, DN_QKV), lambda b, s: (0, 0)),
                  pl.BlockSpec((1, LORA_PAD), lambda b, s: (0, 0)),
                  pl.BlockSpec((1, LORA_PAD), lambda b, s: (0, 0)),
                  pl.BlockSpec((1, DN_HEAD), lambda b, s: (0, 0))],
        out_specs=pl.BlockSpec((CHUNK, DN_VAL), lambda b, s: (b * ns + s, 0)),
        scratch_shapes=[pltpu.VMEM((HALO_ROWS + CHUNK, DN_QKV), F32),
                        pltpu.VMEM((DN_HEADS, DN_HEAD, DN_HEAD), F32)],
        compiler_params=_params(2),
        name="gated_deltanet",
    )(p_dn, p_dn, conv_w, alog, dtb, norm_w.reshape(1, DN_HEAD))


def _rwkv_kernel(p_ref, halo_ref, mu_ref, w0_ref, a0_ref, kk_ref, ka_ref, rk_ref, lnw_ref, lnb_ref,
                 w2_ref, a2_ref, g2_ref, o_ref, state_ref):
    s = pl.program_id(1)

    @pl.when(s == 0)
    def _():
        state_ref[...] = jnp.zeros_like(state_ref)

    row1 = lax.broadcasted_iota(jnp.int32, (CHUNK, 1), 0)

    def shifted(lo, hi):
        cur = p_ref[:, lo:hi]
        last = jnp.where(s == 0, 0.0, halo_ref[HALO_ROWS - 1:HALO_ROWS, lo:hi])
        prev = jnp.where(row1 == 0, last, pltpu.roll(cur, 1, 0))
        return cur + (prev - cur) * mu_ref[:, lo:hi]

    wd = shifted(RW_OFF_WD, RW_OFF_WD + LORA_PAD)
    ad = shifted(RW_OFF_AD, RW_OFF_AD + LORA_PAD)
    gd = shifted(RW_OFF_GD, RW_OFF_GD + RW_GATE_LORA)
    log_w = -RW_DECAY_SCALE * _sigmoid(w0_ref[...] + _dot(jnp.tanh(wd), w2_ref[...]))
    a_all = _sigmoid(a0_ref[...] + _dot(ad, a2_ref[...]))
    g_all = _dot(_sigmoid(gd), g2_ref[...])

    lane = lax.broadcasted_iota(jnp.int32, (CHUNK, LANES), 1)
    row = lax.broadcasted_iota(jnp.int32, (CHUNK, LANES), 0)
    tcol = lane % RW_HEAD
    strict_ss = row > tcol
    incl_ss = row >= tcol
    eye_ss = jnp.where(row == tcol, 1.0, 0.0)
    r2 = lax.broadcasted_iota(jnp.int32, (LANES, LANES), 0)
    c2 = lax.broadcasted_iota(jnp.int32, (LANES, LANES), 1)
    bd_mask = (r2 // RW_HEAD) == (c2 // RW_HEAD)
    ones_bd = bd_mask.astype(BF16)
    head0 = lane < RW_HEAD
    eye = _eye_bf16(LANES)

    def seg_sum(x):
        s0 = jnp.sum(jnp.where(head0, x, 0.0), axis=-1, keepdims=True)
        s1 = jnp.sum(jnp.where(head0, 0.0, x), axis=-1, keepdims=True)
        return jnp.where(head0, s0, s1)

    def block_diag(x):
        return jnp.where(bd_mask, jnp.concatenate([x, x], axis=0), 0.0)

    def per_head_rows(x):
        return jnp.concatenate([jnp.where(head0, x, 0.0), jnp.where(head0, 0.0, x)], axis=0)

    def run_pairs(pairs):
        sl = {p: slice(p * LANES, (p + 1) * LANES) for p in pairs}
        r = {p: shifted(RW_OFF_R + p * LANES, RW_OFF_R + (p + 1) * LANES) for p in pairs}
        k = {p: shifted(RW_OFF_K + p * LANES, RW_OFF_K + (p + 1) * LANES) for p in pairs}
        v = {p: shifted(RW_OFF_V + p * LANES, RW_OFF_V + (p + 1) * LANES) for p in pairs}
        lw = {p: log_w[:, sl[p]] for p in pairs}
        a = {p: a_all[:, sl[p]] for p in pairs}
        kkr = {p: k[p] * kk_ref[:, sl[p]] for p in pairs}
        kss = {p: seg_sum(kkr[p] * kkr[p]) for p in pairs}
        gc = {p: _cumsum_rows(lw[p]) for p in pairs}
        kk = {p: kkr[p] * lax.rsqrt(kss[p] + NORM_EPS) for p in pairs}
        kh = {p: k[p] * (1.0 + (a[p] - 1.0) * ka_ref[:, sl[p]]) for p in pairs}
        b = {p: kk[p] * a[p] for p in pairs}
        bonus_s = {p: seg_sum(r[p] * kh[p] * rk_ref[:, sl[p]]) for p in pairs}
        glast = {p: gc[p][CHUNK - 1:CHUNK, :] for p in pairs}
        e_neg = {p: jnp.exp(-gc[p]) for p in pairs}
        e_tail = {p: jnp.exp(glast[p] - gc[p]) for p in pairs}
        x1 = {p: jnp.concatenate([kk[p] * jnp.exp(gc[p] - lw[p]), r[p] * jnp.exp(gc[p])], axis=0) for p in pairs}
        x3 = {p: jnp.concatenate([kh[p] * e_tail[p], b[p] * e_tail[p]], axis=0) for p in pairs}
        st = {p: state_ref[p] for p in pairs}
        akb = {p: _dot_nt(x1[p], jnp.concatenate([per_head_rows(kh[p] * e_neg[p]),
                                                  per_head_rows(b[p] * e_neg[p]), st[p]], axis=0)) for p in pairs}
        ak = {p: akb[p][:, 0:LANES] for p in pairs}
        ab = {p: akb[p][:, LANES:2 * LANES] for p in pairs}
        xh = {p: akb[p][:, 2 * LANES:3 * LANES] for p in pairs}
        bd_v = {p: block_diag(v[p]) for p in pairs}
        rhs = {p: xh[p][:CHUNK] + _dot(jnp.where(strict_ss, ak[p][:CHUNK], 0.0), bd_v[p]) for p in pairs}
        m = {p: -jnp.where(strict_ss, ab[p][:CHUNK], 0.0) for p in pairs}
        inv = {p: eye_ss + m[p] for p in pairs}
        m = {p: _dot(m[p], block_diag(m[p])) for p in pairs}
        for _ in range(4):
            both = {p: _dot(jnp.concatenate([m[p], inv[p]], axis=0), block_diag(m[p])) for p in pairs}
            inv = {p: inv[p] + both[p][CHUNK:] for p in pairs}
            m = {p: both[p][:CHUNK] for p in pairs}
        inv = {p: inv[p] + _dot(inv[p], block_diag(m[p])) for p in pairs}
        u = {p: _dot(inv[p], block_diag(rhs[p])) for p in pairs}
        o = {p: xh[p][CHUNK:] + _dot(jnp.concatenate([jnp.where(incl_ss, ak[p][CHUNK:], 0.0),
                                                      jnp.where(incl_ss, ab[p][CHUNK:], 0.0)], axis=1),
                                     jnp.concatenate([bd_v[p], -block_diag(u[p])], axis=0)) for p in pairs}
        vu_t = {p: jnp.concatenate([v[p], -u[p]], axis=0).T for p in pairs}
        for p in pairs:
            state_ref[p] = st[p] * jnp.exp(glast[p]) + jnp.where(bd_mask, _dot(vu_t[p], x3[p]), 0.0)
        oc = {p: o[p] - seg_sum(o[p]) * (1.0 / RW_HEAD) for p in pairs}
        var = {p: seg_sum(oc[p] * oc[p]) * (1.0 / RW_HEAD) for p in pairs}
        for p in pairs:
            on = oc[p] * lax.rsqrt(var[p] + RW_GN_EPS) * lnw_ref[:, sl[p]] + lnb_ref[:, sl[p]]
            o_ref[:, sl[p]] = ((on + bonus_s[p] * v[p]) * g_all[:, sl[p]]).astype(o_ref.dtype)

    n_pairs = RW_HEADS // 2
    for grp in range(n_pairs // RW_PAIR_GROUP):
        run_pairs(range(grp * RW_PAIR_GROUP, (grp + 1) * RW_PAIR_GROUP))


def _rwkv7(p_rw, batch, seq, mu, w0, w2, a0, a2, g2, k_k, k_a, r_k, ln_w, ln_b):
    t = batch * seq
    ns = seq // CHUNK
    hb = CHUNK // HALO_ROWS
    vec = lambda x: x.reshape(1, RW_WIDTH)
    cvec = pl.BlockSpec((1, RW_WIDTH), lambda b, s: (0, 0))
    return pl.pallas_call(
        _rwkv_kernel,
        out_shape=jax.ShapeDtypeStruct((t, RW_WIDTH), BF16),
        grid=(batch, ns),
        in_specs=[pl.BlockSpec((CHUNK, RW_PCOLS), lambda b, s: (b * ns + s, 0)),
                  pl.BlockSpec((HALO_ROWS, RW_PCOLS), lambda b, s: (jnp.maximum((b * ns + s) * hb - 1, 0), 0)),
                  pl.BlockSpec((1, RW_PCOLS), lambda b, s: (0, 0)),
                  cvec, cvec, cvec, cvec, cvec, cvec, cvec,
                  pl.BlockSpec((LORA_PAD, RW_WIDTH), lambda b, s: (0, 0)),
                  pl.BlockSpec((LORA_PAD, RW_WIDTH), lambda b, s: (0, 0)),
                  pl.BlockSpec((RW_GATE_LORA, RW_WIDTH), lambda b, s: (0, 0))],
        out_specs=pl.BlockSpec((CHUNK, RW_WIDTH), lambda b, s: (b * ns + s, 0)),
        scratch_shapes=[pltpu.VMEM((RW_HEADS // 2, LANES, LANES), F32)],
        compiler_params=_params(2),
        name="rwkv7",
    )(p_rw, p_rw, mu.reshape(1, RW_PCOLS), vec(w0), vec(a0), vec(k_k), vec(k_a), vec(r_k), vec(ln_w),
      vec(ln_b), w2, a2, g2)


PACK_ROWS = D_MODEL // (2 * LANES)
PACK_PITCH = PACK_ROWS + 2


def _pack_pair(a, b):
    ua = lax.bitcast_convert_type(a.astype(BF16).astype(F32), jnp.uint32)
    ub = lax.bitcast_convert_type(b.astype(BF16).astype(F32), jnp.uint32)
    return ua | lax.shift_right_logical(ub, jnp.uint32(16))


def _unpack_pair(u):
    a = lax.bitcast_convert_type(u & jnp.uint32(0xFFFF0000), F32)
    b = lax.bitcast_convert_type(lax.shift_left(u, jnp.uint32(16)), F32)
    return a, b


def _store_packed(ref, rows, x):
    half = D_MODEL // 2
    for j in range(PACK_ROWS):
        ref[pl.ds(j, rows, stride=PACK_PITCH), :] = _pack_pair(x[:, j * LANES:(j + 1) * LANES],
                                                               x[:, half + j * LANES:half + (j + 1) * LANES])
    for j in range(PACK_ROWS, PACK_PITCH):
        ref[pl.ds(j, rows, stride=PACK_PITCH), :] = jnp.zeros((rows, LANES), jnp.uint32)


def _router_kernel(x_ref, g_ref, wr_ref, br_ref, xpk_ref, eid_ref, wgt_ref):
    x = x_ref[...]
    xn = x * lax.rsqrt(jnp.mean(x * x, axis=-1, keepdims=True) + NORM_EPS) * g_ref[...]
    _store_packed(xpk_ref, x.shape[0], xn)
    xh, xl = _split2(xn)
    wh, wl = _split2(wr_ref[...])
    logits = (jnp.dot(xh, wh, preferred_element_type=F32) + jnp.dot(xh, wl, preferred_element_type=F32)
              + jnp.dot(xl, wh, preferred_element_type=F32)) + br_ref[...]
    tm = logits.shape[0]
    lane = lax.broadcasted_iota(jnp.int32, (tm, LANES), 1)
    neg = -1e30
    big = 4 * LANES

    def first_argmax(vals):
        mx = jnp.max(vals, axis=-1, keepdims=True)
        idx = jnp.min(jnp.where(vals == mx, lane, big), axis=-1, keepdims=True)
        return mx, idx

    gl = jnp.where(lane < N_GROUPS, logits, neg)
    gmax, gi = first_argmax(gl)
    gp = 1.0 / jnp.sum(jnp.where(lane < N_GROUPS, jnp.exp(gl - gmax), 0.0), axis=-1, keepdims=True)
    e_lo = N_GROUPS + gi * EXPERTS_PER_GROUP
    el = jnp.where((lane >= e_lo) & (lane < e_lo + EXPERTS_PER_GROUP), logits, neg)
    m1, i1 = first_argmax(el)
    m2, i2 = first_argmax(jnp.where(lane == i1, neg, el))
    e2 = jnp.exp(m2 - m1)
    w1 = gp / (1.0 + e2)
    w2 = gp * e2 / (1.0 + e2)
    eid_ref[...] = jnp.where(lane == 0, i1 - N_GROUPS, jnp.where(lane == 1, i2 - N_GROUPS, 0))
    wgt_ref[...] = jnp.where(lane == 0, w1, jnp.where(lane == 1, w2, 0.0))


def _router(x_mid, norm_g, wr, br, tm=256):
    t, d = x_mid.shape
    tm = min(tm, t)
    return pl.pallas_call(
        _router_kernel,
        out_shape=(jax.ShapeDtypeStruct((t * PACK_PITCH, LANES), jnp.uint32),
                   jax.ShapeDtypeStruct((t, LANES), jnp.int32),
                   jax.ShapeDtypeStruct((t, LANES), F32)),
        grid=(t // tm,),
        in_specs=[pl.BlockSpec((tm, d), lambda i: (i, 0)), pl.BlockSpec((1, d), lambda i: (0, 0)),
                  pl.BlockSpec((d, LANES), lambda i: (0, 0)), pl.BlockSpec((1, LANES), lambda i: (0, 0))],
        out_specs=(pl.BlockSpec((tm * PACK_PITCH, LANES), lambda i: (i, 0)),
                   pl.BlockSpec((tm, LANES), lambda i: (i, 0)), pl.BlockSpec((tm, LANES), lambda i: (i, 0))),
        compiler_params=_params(1),
        name="router",
    )(x_mid, norm_g.reshape(1, d), wr, br)


DMA_GROUP = 8
RANK_BLOCK = 128


def _expert_changed(be_ref, i):
    return jnp.logical_or(i == 0, be_ref[i] != be_ref[jnp.maximum(i - 1, 0)])


def _moe_up_kernel(be_ref, nv_ref, row_ref, x_hbm, w1_ref, w3_ref, h_ref, xbuf, xb, wb1, wb3, sem):
    i = pl.program_id(0)
    j = pl.program_id(1)
    nb = pl.num_programs(0)
    slot = i % 2
    half = D_MODEL // 2
    rows = h_ref.shape[0]

    def gather(blk, sl, wait):
        n = nv_ref[blk]
        ng = lax.shift_right_logical(n, 3)

        def row_copy(r):
            src = row_ref[blk * rows + r]
            return pltpu.make_async_copy(x_hbm.at[pl.ds(src, PACK_ROWS)],
                                         xbuf.at[sl, pl.ds(r * PACK_PITCH, PACK_ROWS)], sem.at[sl])

        def group(g, carry):
            if wait:
                n_rows = DMA_GROUP * PACK_ROWS
                pltpu.make_async_copy(x_hbm.at[pl.ds(0, n_rows)], xbuf.at[sl, pl.ds(0, n_rows)],
                                      sem.at[sl]).wait()
            else:
                for q in range(DMA_GROUP):
                    row_copy(g * DMA_GROUP + q).start()
            return carry

        def single(r, carry):
            if wait:
                row_copy(r).wait()
            else:
                row_copy(r).start()
            return carry

        lax.fori_loop(0, ng, group, 0)
        lax.fori_loop(ng * DMA_GROUP, n, single, 0)

    @pl.when(jnp.logical_and(i == 0, j == 0))
    def _():
        xbuf[...] = jnp.zeros_like(xbuf)
        gather(0, 0, False)

    @pl.when(jnp.logical_and(j == 0, i + 1 < nb))
    def _():
        gather(jnp.minimum(i + 1, nb - 1), 1 - slot, False)

    @pl.when(j == 0)
    def _():
        gather(i, slot, True)

    @pl.when(jnp.logical_and(j == 0, nv_ref[i] > 0))
    def _():
        xs = xbuf.at[slot]
        for c in range(PACK_ROWS):
            hi, lo = _unpack_pair(xs[pl.ds(c, rows, stride=PACK_PITCH), :])
            xb[:, c * LANES:(c + 1) * LANES] = hi.astype(BF16)
            xb[:, half + c * LANES:half + (c + 1) * LANES] = lo.astype(BF16)

    @pl.when(jnp.logical_and(_expert_changed(be_ref, i), nv_ref[i] > 0))
    def _():
        wb1[j] = w1_ref[0].astype(BF16)
        wb3[j] = w3_ref[0].astype(BF16)

    def up_rows(m):
        x = xb[0:m, :]
        h1 = jnp.dot(x, wb1[j], preferred_element_type=F32)
        h3 = jnp.dot(x, wb3[j], preferred_element_type=F32)
        h_ref[0:m, :] = (_silu(h1) * h3).astype(h_ref.dtype)

    @pl.when(nv_ref[i] > rows // 2)
    def _():
        up_rows(rows)

    @pl.when(jnp.logical_and(nv_ref[i] > 0, nv_ref[i] <= rows // 2))
    def _():
        up_rows(rows // 2)
        h_ref[rows // 2:, :] = jnp.zeros((rows - rows // 2, h_ref.shape[1]), h_ref.dtype)

    @pl.when(nv_ref[i] == 0)
    def _():
        h_ref[...] = jnp.zeros_like(h_ref)


def _moe_down_kernel(be_ref, nv_ref, dst_ref, h_ref, w2_ref, y_hbm, ybuf, wb2, sem):
    i = pl.program_id(0)
    nb = pl.num_programs(0)
    slot = i % 2
    rows = h_ref.shape[0]

    def row_copy(blk, sl, r):
        d = dst_ref[blk * rows + r]
        return pltpu.make_async_copy(ybuf.at[sl, pl.ds(r * PACK_PITCH, PACK_PITCH)],
                                     y_hbm.at[pl.ds(d, PACK_PITCH)], sem.at[sl])

    def scatter(blk, sl, wait):
        n = nv_ref[blk]
        ng = lax.shift_right_logical(n, 3)

        def group(g, carry):
            if wait:
                n_rows = DMA_GROUP * PACK_PITCH
                r0 = pl.multiple_of(g * n_rows, 8)
                pltpu.make_async_copy(ybuf.at[sl, pl.ds(r0, n_rows)], y_hbm.at[pl.ds(0, n_rows)],
                                      sem.at[sl]).wait()
            else:
                for j in range(DMA_GROUP):
                    row_copy(blk, sl, g * DMA_GROUP + j).start()
            return carry

        def single(r, carry):
            cp = row_copy(blk, sl, r)
            if wait:
                cp.wait()
            else:
                cp.start()
            return carry

        lax.fori_loop(0, ng, group, 0)
        lax.fori_loop(ng * DMA_GROUP, n, single, 0)

    prev2 = jnp.maximum(i - 2, 0)

    @pl.when(i >= 2)
    def _():
        scatter(prev2, slot, True)

    @pl.when(nv_ref[i] > 0)
    def _():
        @pl.when(_expert_changed(be_ref, i))
        def _():
            wb2[...] = w2_ref[0].astype(BF16)

        _store_packed(ybuf.at[slot], rows, jnp.dot(h_ref[...], wb2[...], preferred_element_type=F32))
        scatter(i, slot, False)

    prev1 = jnp.maximum(i - 1, 0)

    @pl.when(jnp.logical_and(i == nb - 1, i >= 1))
    def _():
        scatter(prev1, 1 - slot, True)

    @pl.when(i == nb - 1)
    def _():
        scatter(i, slot, True)


def _moe_experts(xpk, t, eid, w1, w3, w2):
    d = D_MODEL
    a = t * TOP_K
    e_flat = eid.reshape(a)
    tok_flat = jnp.repeat(jnp.arange(t, dtype=jnp.int32), TOP_K)
    dst_flat = (jnp.arange(a, dtype=jnp.int32) % TOP_K) * t + tok_flat
    onehot = (e_flat[:, None] == jnp.arange(N_EXPERTS, dtype=jnp.int32)[None, :]).astype(F32)
    onehot = onehot.reshape(a // RANK_BLOCK, RANK_BLOCK, N_EXPERTS)
    tri = jnp.tril(jnp.ones((RANK_BLOCK, RANK_BLOCK), F32))
    within = jnp.einsum('ij,bjk->bik', tri, onehot)
    block_tot = within[:, -1, :]
    block_off = jnp.cumsum(block_tot, axis=0) - block_tot
    rank = (jnp.sum((within + block_off[:, None, :]) * onehot, axis=-1).reshape(a) - 1.0).astype(jnp.int32)
    counts = (block_off[-1] + block_tot[-1]).astype(jnp.int32)
    pcounts = (counts + MOE_PAD - 1) // MOE_PAD * MOE_PAD
    pend = jnp.cumsum(pcounts)
    poffs = pend - pcounts
    dest = poffs[e_flat] + rank
    p_rows = a + N_EXPERTS * MOE_PAD
    slot_dst = jnp.zeros((p_rows,), jnp.int32).at[dest].set(dst_flat)
    slot_src = (slot_dst % t) * PACK_PITCH
    slot_dst = slot_dst * PACK_PITCH

    def blocks(rows):
        starts = jnp.arange(p_rows // rows, dtype=jnp.int32) * rows
        block_e = jnp.minimum(jnp.searchsorted(pend, starts, side='right'), N_EXPERTS - 1).astype(jnp.int32)
        block_v = jnp.clip(poffs[block_e] + counts[block_e] - starts, 0, rows).astype(jnp.int32)
        last_e = block_e[jnp.maximum(jnp.sum(starts < pend[-1]) - 1, 0)]
        return jnp.where(starts < pend[-1], block_e, last_e), block_v

    up_e, up_v = blocks(UP_ROWS)

    def w_index(i, j, be, bv, tk):
        fetch = jnp.logical_and(_expert_changed(be, i), bv[i] > 0)
        return be[i], 0, jnp.where(fetch, j, UP_TILES - 1)

    h = pl.pallas_call(
        _moe_up_kernel,
        out_shape=jax.ShapeDtypeStruct((p_rows, D_EXPERT), BF16),
        grid_spec=pltpu.PrefetchScalarGridSpec(
            num_scalar_prefetch=3,
            grid=(p_rows // UP_ROWS, UP_TILES),
            in_specs=[pl.BlockSpec(memory_space=pl.ANY), pl.BlockSpec((1, d, UP_TILE), w_index),
                      pl.BlockSpec((1, d, UP_TILE), w_index)],
            out_specs=pl.BlockSpec((UP_ROWS, UP_TILE), lambda i, j, be, bv, tk: (i, j)),
            scratch_shapes=[pltpu.VMEM((2, UP_ROWS * PACK_PITCH, LANES), jnp.uint32),
                            pltpu.VMEM((UP_ROWS, d), BF16), pltpu.VMEM((UP_TILES, d, UP_TILE), BF16),
                            pltpu.VMEM((UP_TILES, d, UP_TILE), BF16), pltpu.SemaphoreType.DMA((2,))]),
        compiler_params=_params(2),
        name="moe_up",
    )(up_e, up_v, slot_src, xpk, w1, w3)

    down_e, down_v = blocks(DOWN_ROWS)
    y = pl.pallas_call(
        _moe_down_kernel,
        out_shape=jax.ShapeDtypeStruct((TOP_K * t * PACK_PITCH, LANES), jnp.uint32),
        grid_spec=pltpu.PrefetchScalarGridSpec(
            num_scalar_prefetch=3,
            grid=(p_rows // DOWN_ROWS,),
            in_specs=[pl.BlockSpec((DOWN_ROWS, D_EXPERT), lambda i, be, bv, ds: (i, 0)),
                      pl.BlockSpec((1, D_EXPERT, d), lambda i, be, bv, ds: (be[i], 0, 0))],
            out_specs=pl.BlockSpec(memory_space=pl.ANY),
            scratch_shapes=[pltpu.VMEM((2, DOWN_ROWS * PACK_PITCH, LANES), jnp.uint32),
                            pltpu.VMEM((D_EXPERT, d), BF16), pltpu.SemaphoreType.DMA((2,))]),
        compiler_params=_params(1),
        name="moe_down",
    )(down_e, down_v, slot_dst, h, w2)
    return y


def _final_kernel(x_ref, y0_ref, y1_ref, wgt_ref, g_ref, o_ref):
    tm = x_ref.shape[0]
    half = D_MODEL // 2
    w0 = wgt_ref[:, 0:1]
    w1 = wgt_ref[:, 1:2]
    ss = jnp.zeros((tm, 1), F32)
    for j in range(PACK_ROWS):
        a0, b0 = _unpack_pair(y0_ref[pl.ds(j, tm, stride=PACK_PITCH), :])
        a1, b1 = _unpack_pair(y1_ref[pl.ds(j, tm, stride=PACK_PITCH), :])
        ca = slice(j * LANES, (j + 1) * LANES)
        cb = slice(half + j * LANES, half + (j + 1) * LANES)
        za = x_ref[:, ca] + w0 * a0 + w1 * a1
        zb = x_ref[:, cb] + w0 * b0 + w1 * b1
        o_ref[:, ca] = za
        o_ref[:, cb] = zb
        ss = ss + jnp.sum(za * za, axis=-1, keepdims=True) + jnp.sum(zb * zb, axis=-1, keepdims=True)
    o_ref[...] = o_ref[...] * lax.rsqrt(ss * (1.0 / D_MODEL) + NORM_EPS) * g_ref[...]


def _final(x_mid, y, wgt, g, tm=256):
    t, d = x_mid.shape
    tm = min(tm, t)
    nblk = t // tm
    return pl.pallas_call(
        _final_kernel,
        out_shape=jax.ShapeDtypeStruct((t, d), F32),
        grid=(nblk,),
        in_specs=[pl.BlockSpec((tm, d), lambda i: (i, 0)),
                  pl.BlockSpec((tm * PACK_PITCH, LANES), lambda i: (i, 0)),
                  pl.BlockSpec((tm * PACK_PITCH, LANES), lambda i: (i + nblk, 0)),
                  pl.BlockSpec((tm, LANES), lambda i: (i, 0)), pl.BlockSpec((1, d), lambda i: (0, 0))],
        out_specs=pl.BlockSpec((tm, d), lambda i: (i, 0)),
        compiler_params=_params(1),
        name="final_combine",
    )(x_mid, y, y, wgt, g.reshape(1, d))


def _pad_cols(w, n):
    return jnp.pad(w, ((0, 0), (0, n - w.shape[1])))


def _pad_rows(w, n):
    return jnp.pad(w, ((0, n - w.shape[0]), (0, 0)))


def _layer(x2, batch, seq, norm1_g, w_in, rw_mu, rw_w0, rw_w2, rw_a0, rw_a2, rw_g2, rw_k_k, rw_k_a, rw_r_k,
           rw_ln_w, rw_ln_b, dn_conv_w, dn_a_log, dn_dt_bias, dn_norm_w, gate_b, w_branch, w_out, norm2_g,
           moe_gr_w, moe_gr_b, moe_er_w, moe_er_b, moe_w1, moe_w3, moe_w2):
    sizes = (RW_WIDTH, RW_DECAY_LORA, RW_WIDTH, RW_WIDTH, RW_A_LORA, RW_GATE_LORA,
             DN_QKV, 2 * DN_HEADS, DN_VAL, 2 * D_MODEL)
    offs = [0]
    for sz in sizes:
        offs.append(offs[-1] + sz)
    cols = lambda w, i: w[..., offs[i]:offs[i + 1]]
    w_bf = w_in.astype(BF16)
    w_rw = jnp.concatenate([cols(w_bf, 0), _pad_cols(cols(w_bf, 1), LORA_PAD), cols(w_bf, 2), cols(w_bf, 3),
                            _pad_cols(cols(w_bf, 4), LORA_PAD), cols(w_bf, 5)], axis=1)
    w_dn = _pad_cols(jnp.concatenate([cols(w_bf, 6), _pad_cols(cols(w_bf, 7), LORA_PAD), cols(w_bf, 8)], axis=1),
                     DN_PCOLS)
    w_gate = cols(w_bf, 9)
    mu2 = rw_mu.reshape(1, -1)
    mu = jnp.concatenate([cols(mu2, 0), _pad_cols(cols(mu2, 1), LORA_PAD), cols(mu2, 2), cols(mu2, 3),
                          _pad_cols(cols(mu2, 4), LORA_PAD), cols(mu2, 5)], axis=1)

    h = _rmsnorm(x2, norm1_g, BF16)
    p_rw = _matmul(h, w_rw, F32, 1024, 512, "proj_rwkv")
    p_dn = _matmul(h, w_dn, F32, 1024, 3 * MXU_WIDTH, "proj_gdn")
    gates = _matmul(h, w_gate, BF16, 1024, 1024, "proj_gate", bias=gate_b)

    o_rw = _rwkv7(p_rw, batch, seq, mu, rw_w0, _pad_rows(rw_w2, LORA_PAD).astype(BF16), rw_a0,
                  _pad_rows(rw_a2, LORA_PAD).astype(BF16), rw_g2.astype(BF16), rw_k_k, rw_k_a,
                  rw_r_k.reshape(-1), rw_ln_w, rw_ln_b)
    o_dn = _gated_deltanet(p_dn, batch, seq, dn_conv_w, dn_a_log, dn_dt_bias, dn_norm_w)

    mix = _branch_mix(o_rw, o_dn, w_branch[:RW_WIDTH].astype(BF16), w_branch[RW_WIDTH:].astype(BF16), gates,
                      tn=1024)
    x_mid = _matmul(mix, w_out.astype(BF16), F32, 1024, 1024, "proj_out", residual=x2)

    wr = _pad_cols(jnp.concatenate([moe_gr_w, moe_er_w], axis=1), LANES)
    br = _pad_cols(jnp.concatenate([moe_gr_b, moe_er_b]).reshape(1, -1), LANES)
    xpk, eid, wgt = _router(x_mid, norm2_g, wr, br)
    y = _moe_experts(xpk, x_mid.shape[0], eid[:, :TOP_K], moe_w1, moe_w3, moe_w2)
    return x_mid, y, wgt


def kernel(x, norm1_g, w_in, rw_mu, rw_w0, rw_w2, rw_a0, rw_a2, rw_g2, rw_k_k, rw_k_a, rw_r_k, rw_ln_w, rw_ln_b, dn_conv_w, dn_a_log, dn_dt_bias, dn_norm_w, gate_b, w_branch, w_out, norm2_g, moe_gr_w, moe_gr_b, moe_er_w, moe_er_b, moe_w1, moe_w3, moe_w2, final_g):
    batch, seq, d = x.shape
    assert norm1_g.shape[0] == 1, "single-layer block"
    x_mid, y, wgt = _layer(
        x.reshape(batch * seq, d), batch, seq, norm1_g[0], w_in[0], rw_mu[0], rw_w0[0], rw_w2[0], rw_a0[0],
        rw_a2[0], rw_g2[0], rw_k_k[0], rw_k_a[0], rw_r_k[0], rw_ln_w[0], rw_ln_b[0], dn_conv_w[0], dn_a_log[0],
        dn_dt_bias[0], dn_norm_w[0], gate_b[0], w_branch[0], w_out[0], norm2_g[0], moe_gr_w[0], moe_gr_b[0],
        moe_er_w[0], moe_er_b[0], moe_w1[0], moe_w3[0], moe_w2[0])
    return _final(x_mid, y, wgt, final_g).reshape(batch, seq, d)
```
